```python
import math
import jax
import jax.numpy as jnp
from jax import lax
import numpy as np

D_MODEL = 1024
BATCH = 16
SEQ = 2048
DEPTH = 4

CHUNK = 64
Q_BLOCK = 128
ROPE_THETA = 10000.0
NORM_EPS = 1e-6

A_HEADS = 4
A_QK_DIM = 64
A_V_DIM = 2 * A_QK_DIM
B_HEADS = 8
B_Q_RANK = 256
B_KV_RANK = 128
B_NOPE_DIM = 64
B_ROPE_DIM = 32
B_V_DIM = 64
C_HEADS = 16
C_HEAD_DIM = 64
C_WIDTH = C_HEADS * C_HEAD_DIM

A_Q_COLS = A_HEADS * 2 * A_QK_DIM
A_V_COLS = A_HEADS * A_V_DIM
AB_IN_SPLITS = (A_Q_COLS, A_Q_COLS, A_V_COLS, B_Q_RANK, B_KV_RANK, B_ROPE_DIM)
AB_IN_DIM = sum(AB_IN_SPLITS)
AB_OUT_DIM = A_HEADS * A_V_DIM + B_HEADS * B_V_DIM

D_FF = ((8 * D_MODEL // 3 + 255) // 256) * 256
N_EVEN = (DEPTH + 1) // 2
N_ODD = DEPTH // 2

kernel_name = "hybrid_diff_mla_stickbreak_trunk"


def lambda_init(layer_idx):
    return 0.8 - 0.6 * math.exp(-0.3 * layer_idx)


def rms_norm(x, gain):
    xf = x.astype(jnp.float32)
    y = xf * lax.rsqrt(jnp.mean(xf * xf, axis=-1, keepdims=True) + NORM_EPS)
    return (y * gain.astype(jnp.float32)).astype(x.dtype)


def rope(x, pos):
    d = x.shape[-1]
    half = d // 2
    inv_freq = jnp.power(ROPE_THETA, -jnp.arange(half, dtype=jnp.float32) / half)
    ang = pos.astype(jnp.float32)[:, None] * inv_freq[None, :]
    bshape = (1, x.shape[1]) + (1,) * (x.ndim - 3) + (half,)
    cos = jnp.cos(ang).reshape(bshape)
    sin = jnp.sin(ang).reshape(bshape)
    xf = x.astype(jnp.float32)
    x1, x2 = xf[..., :half], xf[..., half:]
    return jnp.concatenate([x1 * cos - x2 * sin, x2 * cos + x1 * sin], axis=-1).astype(x.dtype)


def split_cols(x, sizes):
    idx = [int(i) for i in np.cumsum(sizes)[:-1]]
    return jnp.split(x, idx, axis=-1)


def sweep_query_blocks(block_fn, seq):
    return jnp.concatenate([block_fn(lo, lo + Q_BLOCK) for lo in range(0, seq, Q_BLOCK)], axis=1)


def chunk_causal_mask(lo, hi):
    q_chunk = jnp.arange(lo, hi) // CHUNK
    k_chunk = jnp.arange(hi) // CHUNK
    return k_chunk[None, :] <= q_chunk[:, None]


def masked_softmax(scores, mask):
    return jax.nn.softmax(jnp.where(mask, scores, -jnp.inf), axis=-1)


def differential_attention(q1, q2, k1, k2, v, lam, scale):
    def block(lo, hi):
        mask = chunk_causal_mask(lo, hi)
        s1 = jnp.einsum('bqhd,bkhd->bhqk', q1[:, lo:hi], k1[:, :hi]).astype(jnp.float32) * scale
        s2 = jnp.einsum('bqhd,bkhd->bhqk', q2[:, lo:hi], k2[:, :hi]).astype(jnp.float32) * scale
        p = masked_softmax(s1, mask) - lam * masked_softmax(s2, mask)
        return jnp.einsum('bhqk,bkhd->bqhd', p.astype(v.dtype), v[:, :hi])
    return sweep_query_blocks(block, q1.shape[1])


def latent_attention(q_nope, q_rope, k_nope, k_rope, v, scale):
    def block(lo, hi):
        mask = chunk_causal_mask(lo, hi)
        s = (jnp.einsum('bqhd,bkhd->bhqk', q_nope[:, lo:hi], k_nope[:, :hi])
             + jnp.einsum('bqhr,bkr->bhqk', q_rope[:, lo:hi], k_rope[:, :hi])).astype(jnp.float32) * scale
        p = masked_softmax(s, mask)
        return jnp.einsum('bhqk,bkhd->bqhd', p.astype(v.dtype), v[:, :hi])
    return sweep_query_blocks(block, q_nope.shape[1])


def stick_breaking_attention(q, k, v, scale):
    def block(lo, hi):
        q_pos = jnp.arange(lo, hi)
        k_pos = jnp.arange(hi)
        mask = k_pos[None, :] < q_pos[:, None]
        z = jnp.einsum('bqhd,bkhd->bhqk', q[:, lo:hi], k[:, :hi]).astype(jnp.float32) * scale
        sp = jnp.where(mask, jax.nn.softplus(z), 0.0)
        tail = lax.cumsum(sp, axis=3, reverse=True) - sp
        w = jnp.where(mask, jnp.exp(jax.nn.log_sigmoid(z) - tail), 0.0)
        return jnp.einsum('bhqk,bkhd->bqhd', w.astype(v.dtype), v[:, :hi])
    return sweep_query_blocks(block, q.shape[1])


def diff_mla_mixer(h, pos, layer_idx, w_in, a_q_norm, a_k_norm, a_lambda, a_out_norm,
                   b_q_a_norm, b_w_q_b, b_kv_a_norm, b_w_kv_b,
                   b_q_nope_norm, b_q_rope_norm, b_k_nope_norm, b_k_rope_norm, w_out):
    B, S, _ = h.shape
    a_q, a_k, a_v, b_ql, b_kvl, b_kr = split_cols(h @ w_in, AB_IN_SPLITS)

    a_q = rope(rms_norm(a_q.reshape(B, S, 2 * A_HEADS, A_QK_DIM), a_q_norm), pos)
    a_k = rope(rms_norm(a_k.reshape(B, S, 2 * A_HEADS, A_QK_DIM), a_k_norm), pos)
    a_q = a_q.reshape(B, S, A_HEADS, 2, A_QK_DIM)
    a_k = a_k.reshape(B, S, A_HEADS, 2, A_QK_DIM)
    a_v = a_v.reshape(B, S, A_HEADS, A_V_DIM)
    lam_f = a_lambda.astype(jnp.float32)
    lam_0 = lambda_init(layer_idx)
    lam = jnp.exp(jnp.sum(lam_f[0] * lam_f[1])) - jnp.exp(jnp.sum(lam_f[2] * lam_f[3])) + lam_0
    out_a = differential_attention(a_q[:, :, :, 0], a_q[:, :, :, 1], a_k[:, :, :, 0], a_k[:, :, :, 1],
                                   a_v, lam, A_QK_DIM ** -0.5)
    out_a = rms_norm(out_a, a_out_norm) * (1.0 - lam_0)

    q_b = (rms_norm(b_ql, b_q_a_norm) @ b_w_q_b).reshape(B, S, B_HEADS, B_NOPE_DIM + B_ROPE_DIM)
    kv_b = (rms_norm(b_kvl, b_kv_a_norm) @ b_w_kv_b).reshape(B, S, B_HEADS, B_NOPE_DIM + B_V_DIM)
    q_nope, q_rope = q_b[..., :B_NOPE_DIM], q_b[..., B_NOPE_DIM:]
    k_nope, b_v = kv_b[..., :B_NOPE_DIM], kv_b[..., B_NOPE_DIM:]
    q_nope = rms_norm(q_nope, b_q_nope_norm)
    k_nope = rms_norm(k_nope, b_k_nope_norm)
    q_rope = rope(rms_norm(q_rope, b_q_rope_norm), pos)
    k_rope = rope(rms_norm(b_kr, b_k_rope_norm), pos)
    out_b = latent_attention(q_nope, q_rope, k_nope, k_rope, b_v,
                             (B_NOPE_DIM + B_ROPE_DIM) ** -0.5)

    mixed = jnp.concatenate([out_a.reshape(B, S, A_HEADS * A_V_DIM),
                             out_b.reshape(B, S, B_HEADS * B_V_DIM)], axis=-1)
    return mixed @ w_out


def stick_breaking_mixer(h, w_in, w_out):
    B, S, _ = h.shape
    q, k, v = jnp.split(h @ w_in, 3, axis=-1)
    q = q.reshape(B, S, C_HEADS, C_HEAD_DIM)
    k = k.reshape(B, S, C_HEADS, C_HEAD_DIM)
    v = v.reshape(B, S, C_HEADS, C_HEAD_DIM)
    out = stick_breaking_attention(q, k, v, C_HEAD_DIM ** -0.5)
    return out.reshape(B, S, C_WIDTH) @ w_out


def swiglu(h, w_gate, w_up, w_down):
    return (jax.nn.silu(h @ w_gate) * (h @ w_up)) @ w_down


def setup_inputs(seed: int = 0) -> dict:
    key = jax.random.key(seed)
    keys = list(jax.random.split(key, 32))

    def dense(shape, fan_in):
        return jax.random.normal(keys.pop(), shape, jnp.float32) * fan_in ** -0.5

    def gain(shape):
        return 1.0 + 0.02 * jax.random.normal(keys.pop(), shape, jnp.float32)

    return {
        "x": jax.random.normal(keys.pop(), (BATCH, SEQ, D_MODEL), jnp.float32),
        "norm_mix": gain((DEPTH, D_MODEL)),
        "norm_ffn": gain((DEPTH, D_MODEL)),
        "ab_w_in": dense((N_EVEN, D_MODEL, AB_IN_DIM), D_MODEL),
        "a_q_norm": gain((N_EVEN, A_QK_DIM)),
        "a_k_norm": gain((N_EVEN, A_QK_DIM)),
        "a_lambda": 0.1 * jax.random.normal(keys.pop(), (N_EVEN, 4, A_QK_DIM), jnp.float32),
        "a_out_norm": gain((N_EVEN, A_V_DIM)),
        "b_q_a_norm": gain((N_EVEN, B_Q_RANK)),
        "b_w_q_b": dense((N_EVEN, B_Q_RANK, B_HEADS * (B_NOPE_DIM + B_ROPE_DIM)), B_Q_RANK),
        "b_kv_a_norm": gain((N_EVEN, B_KV_RANK)),
        "b_w_kv_b": dense((N_EVEN, B_KV_RANK, B_HEADS * (B_NOPE_DIM + B_V_DIM)), B_KV_RANK),
        "b_q_nope_norm": gain((N_EVEN, B_NOPE_DIM)),
        "b_q_rope_norm": gain((N_EVEN, B_ROPE_DIM)),
        "b_k_nope_norm": gain((N_EVEN, B_NOPE_DIM)),
        "b_k_rope_norm": gain((N_EVEN, B_ROPE_DIM)),
        "ab_w_out": dense((N_EVEN, AB_OUT_DIM, D_MODEL), AB_OUT_DIM),
        "c_w_in": dense((N_ODD, D_MODEL, 3 * C_WIDTH), D_MODEL),
        "c_w_out": dense((N_ODD, C_WIDTH, D_MODEL), C_WIDTH),
        "ffn_w_gate": dense((DEPTH, D_MODEL, D_FF), D_MODEL),
        "ffn_w_up": dense((DEPTH, D_MODEL, D_FF), D_MODEL),
        "ffn_w_down": dense((DEPTH, D_FF, D_MODEL), D_FF),
    }


def reference(x, norm_mix, norm_ffn, ab_w_in, a_q_norm, a_k_norm, a_lambda, a_out_norm,
              b_q_a_norm, b_w_q_b, b_kv_a_norm, b_w_kv_b,
              b_q_nope_norm, b_q_rope_norm, b_k_nope_norm, b_k_rope_norm, ab_w_out,
              c_w_in, c_w_out, ffn_w_gate, ffn_w_up, ffn_w_down):
    pos = jnp.arange(x.shape[1], dtype=jnp.int32)
    for l in range(DEPTH):
        h = rms_norm(x, norm_mix[l])
        i = l // 2
        if l % 2 == 0:
            x = x + diff_mla_mixer(h, pos, l, ab_w_in[i], a_q_norm[i], a_k_norm[i], a_lambda[i],
                                   a_out_norm[i], b_q_a_norm[i], b_w_q_b[i], b_kv_a_norm[i],
                                   b_w_kv_b[i], b_q_nope_norm[i], b_q_rope_norm[i],
                                   b_k_nope_norm[i], b_k_rope_norm[i], ab_w_out[i])
        else:
            x = x + stick_breaking_mixer(h, c_w_in[i], c_w_out[i])
        h = rms_norm(x, norm_ffn[l])
        x = x + swiglu(h, ffn_w_gate[l], ffn_w_up[l], ffn_w_down[l])
    return x
```

```python
import functools
import math

import jax
import jax.numpy as jnp
import numpy as np
from jax import lax
from jax.experimental import pallas as pl
from jax.experimental.pallas import tpu as pltpu

D_MODEL = 1024
BATCH = 16
SEQ = 2048
DEPTH = 4
CHUNK = 64
ROPE_THETA = 10000.0
NORM_EPS = 1e-6

A_HEADS = 4
A_QK_DIM = 64
A_V_DIM = 128
B_HEADS = 8
B_Q_RANK = 256
B_KV_RANK = 128
B_NOPE_DIM = 64
B_ROPE_DIM = 32
B_V_DIM = 64
C_HEADS = 16
C_HEAD_DIM = 64
C_WIDTH = C_HEADS * C_HEAD_DIM
D_FF = 2816

LANES = 128
MXU_DIM = 256
TQ = 256
TK = 256
TM = 256
TM_FFN = 512
FF_CHUNK = 256
VMEM_LIMIT_BYTES = 56 * 1024 * 1024
LOG2E = 1.4426950408889634
NEG_BIG = -1e30

F32 = jnp.float32
BF16 = jnp.bfloat16
NT_DIMS = (((1,), (1,)), ((), ()))


def _lambda_init(layer_idx):
    return 0.8 - 0.6 * math.exp(-0.3 * layer_idx)


def _params(n_axes):
    return pltpu.CompilerParams(dimension_semantics=("arbitrary",) * n_axes,
                                vmem_limit_bytes=VMEM_LIMIT_BYTES)


def _resident(shape):
    zeros = (0,) * len(shape)
    return pl.BlockSpec(shape, lambda *_: zeros, pipeline_mode=pl.Buffered(1))


def _rms(x, gain):
    ms = jnp.mean(x * x, axis=1, keepdims=True)
    return x * lax.rsqrt(ms + NORM_EPS) * gain


def _group_mean_sq(y, g_ref):
    outs = []
    for c in range(y.shape[1] // MXU_DIM):
        yc = y[:, c * MXU_DIM:(c + 1) * MXU_DIM]
        outs.append(jnp.dot((yc * yc).astype(BF16), g_ref[...], preferred_element_type=F32))
    return jnp.concatenate(outs, axis=1)


def _group_rms(y, g_ref, gain):
    return y * lax.rsqrt(_group_mean_sq(y, g_ref) + NORM_EPS) * gain


def _rope(y, first_half, half, cos, sin_signed):
    n = y.shape[1]
    partner = jnp.where(first_half, pltpu.roll(y, n - half, 1), pltpu.roll(y, half, 1))
    return y * cos + partner * sin_signed


def _proj_even_kernel(x_ref, gmix_ref, wtok_ref, wavt_ref, wqb_ref, wkb_ref, wbvt_ref,
                      ga_ref, gb_ref, gaq_ref, gak_ref, gqa_ref, gkva_ref, gbq_ref, gbk_ref,
                      gkr_ref, cosa_ref, sina_ref, cosb_ref, sinb_ref,
                      aq_ref, ak_ref, avt_ref, bq_ref, bk_ref, bvt_ref):
    tm = x_ref.shape[0]
    hb = _rms(x_ref[...], gmix_ref[...]).astype(BF16)
    p = jnp.dot(hb, wtok_ref[...], preferred_element_type=F32)

    lane_a = lax.broadcasted_iota(jnp.int32, (tm, 512), 1)
    first_a = (lane_a & 32) == 0
    cosa = jnp.concatenate([cosa_ref[...]] * 4, axis=1)
    sina = jnp.concatenate([sina_ref[...]] * 4, axis=1)
    aq = _rope(_group_rms(p[:, 0:512], ga_ref, gaq_ref[...]), first_a, 32, cosa, sina)
    aq_ref[...] = aq.astype(BF16)
    ak = _rope(_group_rms(p[:, 512:1024], ga_ref, gak_ref[...]), first_a, 32, cosa, sina)
    ak_ref[...] = ak.astype(BF16)
    avt = lax.dot_general(wavt_ref[...], hb, NT_DIMS, preferred_element_type=F32)
    for c in range(tm // TK):
        avt_ref[c] = avt[:, c * TK:(c + 1) * TK].astype(BF16)

    qln = _rms(p[:, 1024:1280], gqa_ref[...]).astype(BF16)
    qb = jnp.dot(qln, wqb_ref[...], preferred_element_type=F32)
    lane_b = lax.broadcasted_iota(jnp.int32, (tm, 1024), 1)
    first_b = (lane_b & 127) < 80
    cosb = jnp.concatenate([cosb_ref[...]] * 8, axis=1)
    sinb = jnp.concatenate([sinb_ref[...]] * 8, axis=1)
    bq = _rope(_group_rms(qb, gb_ref, gbq_ref[...]), first_b, 16, cosb, sinb)
    bq_ref[...] = bq.astype(BF16)

    kvn = _rms(p[:, 1280:1408], gkva_ref[...]).astype(BF16)
    kb = _group_rms(jnp.dot(kvn, wkb_ref[...], preferred_element_type=F32),
                    gb_ref, gbk_ref[...])
    kr = p[:, 1408:1536]
    krn = kr * lax.rsqrt(jnp.sum(kr * kr, axis=1, keepdims=True) * (1.0 / B_ROPE_DIM)
                         + NORM_EPS) * gkr_ref[...]
    lane_r = lax.broadcasted_iota(jnp.int32, (tm, 128), 1)
    krr = _rope(krn, lane_r < 80, 16, cosb_ref[...], sinb_ref[...])
    bk_ref[...] = (kb + jnp.concatenate([krr] * 8, axis=1)).astype(BF16)
    bvt = lax.dot_general(wbvt_ref[...], kvn, NT_DIMS, preferred_element_type=F32)
    for c in range(tm // TK):
        bvt_ref[c] = bvt[:, c * TK:(c + 1) * TK].astype(BF16)


def _proj_even(x3, consts, w):
    n_s = SEQ // TM
    grid = (BATCH * n_s,)
    row = lambda n: pl.BlockSpec((None, TM, n), lambda i: (i // n_s, i % n_s, 0))
    vt = lambda n: pl.BlockSpec((None, TM // TK, n, TK), lambda i: (i // n_s, i % n_s, 0, 0))
    tab = pl.BlockSpec((TM, LANES), lambda i: (i % n_s, 0))
    ins = [x3, w["gmix"], w["wtok"], w["wavt"], w["wqb"], w["wkb"], w["wbvt"],
           consts["ga"], consts["gb"], w["gaq"], w["gak"], w["gqa"], w["gkva"], w["gbq"],
           w["gbk"], w["gkr"], consts["cosa"], consts["sina"], consts["cosb"], consts["sinb"]]
    in_specs = [row(D_MODEL)] + [_resident(a.shape) for a in ins[1:16]] + [tab] * 4
    out_shape = [
        jax.ShapeDtypeStruct((BATCH, SEQ, 512), BF16),
        jax.ShapeDtypeStruct((BATCH, SEQ, 512), BF16),
        jax.ShapeDtypeStruct((BATCH, SEQ // TK, 512, TK), BF16),
        jax.ShapeDtypeStruct((BATCH, SEQ, 1024), BF16),
        jax.ShapeDtypeStruct((BATCH, SEQ, 1024), BF16),
        jax.ShapeDtypeStruct((BATCH, SEQ // TK, 512, TK), BF16),
    ]
    out_specs = [row(512), row(512), vt(512), row(1024), row(1024), vt(512)]
    return pl.pallas_call(_proj_even_kernel, grid=grid, in_specs=in_specs, out_specs=out_specs,
                          out_shape=out_shape, compiler_params=_params(1),
                          name="proj_even")(*ins)


def _proj_odd_kernel(x_ref, gmix_ref, wqk_ref, wvt_ref, q_ref, k_ref, vt_ref):
    tm = x_ref.shape[0]
    hb = _rms(x_ref[...], gmix_ref[...]).astype(BF16)
    qk = jnp.dot(hb, wqk_ref[...], preferred_element_type=F32)
    q_ref[...] = (qk[:, :C_WIDTH] * (C_HEAD_DIM ** -0.5)).astype(BF16)
    k_ref[...] = qk[:, C_WIDTH:].astype(BF16)
    vt = lax.dot_general(wvt_ref[...], hb, NT_DIMS, preferred_element_type=F32)
    for c in range(tm // TK):
        vt_ref[c] = vt[:, c * TK:(c + 1) * TK].astype(BF16)


def _proj_odd(x3, gmix, wqk, wvt):
    n_s = SEQ // TM
    row = lambda n: pl.BlockSpec((None, TM, n), lambda i: (i // n_s, i % n_s, 0))
    vt = pl.BlockSpec((None, TM // TK, C_WIDTH, TK), lambda i: (i // n_s, i % n_s, 0, 0))
    out_shape = [
        jax.ShapeDtypeStruct((BATCH, SEQ, C_WIDTH), BF16),
        jax.ShapeDtypeStruct((BATCH, SEQ, C_WIDTH), BF16),
        jax.ShapeDtypeStruct((BATCH, SEQ // TK, C_WIDTH, TK), BF16),
    ]
    return pl.pallas_call(
        _proj_odd_kernel, grid=(BATCH * n_s,),
        in_specs=[row(D_MODEL), _resident(gmix.shape), _resident(wqk.shape), _resident(wvt.shape)],
        out_specs=[row(C_WIDTH), row(C_WIDTH), vt], out_shape=out_shape,
        compiler_params=_params(1), name="proj_odd")(x3, gmix, wqk, wvt)


def _softmax_tile(k, q, vt, mask, m, l, acc_ref, rows):
    s = lax.dot_general(k, q, NT_DIMS, preferred_element_type=F32)
    if mask is not None:
        s = jnp.where(mask, s, -jnp.inf)
    m_new = jnp.maximum(m, jnp.max(s, axis=0, keepdims=True))
    alpha = jnp.exp2(m - m_new)
    p = jnp.exp2(s - m_new)
    l_new = alpha * l + jnp.sum(p, axis=0, keepdims=True)
    acc_ref[rows] = alpha * acc_ref[rows] + jnp.dot(vt, p.astype(BF16), preferred_element_type=F32)
    return m_new, l_new


def _split_halves(q):
    lane = lax.broadcasted_iota(jnp.int32, q.shape, 1)
    qf = q.astype(F32)
    return (jnp.where(lane < 64, qf, 0.0).astype(BF16), jnp.where(lane >= 64, qf, 0.0).astype(BF16))


def _attn_a_kernel(q_ref, k_ref, vt_ref, lam_ref, gout_ref, o_ref, acc_ref, *, lam0):
    i = pl.program_id(1)
    al = lam_ref[...]
    lam = (jnp.exp(jnp.sum(al[0:1] * al[1:2], axis=1, keepdims=True))
           - jnp.exp(jnp.sum(al[2:3] * al[3:4], axis=1, keepdims=True)) + lam0)
    r = lax.broadcasted_iota(jnp.int32, (TK, TQ), 0)
    c = lax.broadcasted_iota(jnp.int32, (TK, TQ), 1)
    diag_mask = (r >> 6) <= (c >> 6)
    init = (jnp.full((1, TQ), NEG_BIG, F32), jnp.zeros((1, TQ), F32))
    for h in range(A_HEADS):
        cols = slice(LANES * h, LANES * (h + 1))
        qs = _split_halves(q_ref[:, cols])
        acc_ref[...] = jnp.zeros_like(acc_ref)

        def tile(j, carry, mask):
            k = k_ref[pl.ds(pl.multiple_of(j * TK, TK), TK), cols]
            vt = vt_ref[j, cols, :]
            m1, l1 = _softmax_tile(k, qs[0], vt, mask, carry[0], carry[1], acc_ref, slice(0, 128))
            m2, l2 = _softmax_tile(k, qs[1], vt, mask, carry[2], carry[3], acc_ref, slice(128, 256))
            return m1, l1, m2, l2

        carry = lax.fori_loop(0, i, lambda j, cr: tile(j, cr, None), init + init)
        _, l1, _, l2 = tile(i, carry, diag_mask)
        o = acc_ref[0:128] * (1.0 / l1) - lam * (acc_ref[128:256] * (1.0 / l2))
        o_ref[:, cols] = _rms(o.T, gout_ref[...]).astype(BF16)


def _attn_a(aq, ak, avt, a_lambda, gout, lam0):
    n_q = SEQ // TQ
    kern = functools.partial(_attn_a_kernel, lam0=lam0)
    return pl.pallas_call(
        kern, grid=(BATCH, n_q),
        in_specs=[pl.BlockSpec((None, TQ, 512), lambda b, i: (b, i, 0)),
                  pl.BlockSpec((None, SEQ, 512), lambda b, i: (b, 0, 0)),
                  pl.BlockSpec((None, SEQ // TK, 512, TK), lambda b, i: (b, 0, 0, 0)),
                  _resident(a_lambda.shape), _resident(gout.shape)],
        out_specs=pl.BlockSpec((None, TQ, 512), lambda b, i: (b, i, 0)),
        out_shape=jax.ShapeDtypeStruct((BATCH, SEQ, 512), BF16),
        scratch_shapes=[pltpu.VMEM((2 * A_V_DIM, TQ), F32)],
        compiler_params=_params(2), name="attn_a")(aq, ak, avt, a_lambda, gout)


def _attn_b_kernel(q_ref, k_ref, vt_ref, o_ref, acc_ref):
    i = pl.program_id(1)
    r = lax.broadcasted_iota(jnp.int32, (TK, TQ), 0)
    c = lax.broadcasted_iota(jnp.int32, (TK, TQ), 1)
    diag_mask = (r >> 6) <= (c >> 6)
    init = (jnp.full((1, TQ), NEG_BIG, F32), jnp.zeros((1, TQ), F32))
    for hp in range(B_HEADS // 2):
        acc_ref[...] = jnp.zeros_like(acc_ref)

        def tile(j, carry, mask):
            out = []
            for e in range(2):
                h = 2 * hp + e
                cols = slice(LANES * h, LANES * (h + 1))
                k = k_ref[pl.ds(pl.multiple_of(j * TK, TK), TK), cols]
                vt = vt_ref[j, B_V_DIM * h:B_V_DIM * (h + 1), :]
                out += _softmax_tile(k, q_ref[:, cols], vt, mask, carry[2 * e], carry[2 * e + 1],
                                     acc_ref, slice(B_V_DIM * e, B_V_DIM * (e + 1)))
            return tuple(out)

        carry = lax.fori_loop(0, i, lambda j, cr: tile(j, cr, None), init + init)
        _, l1, _, l2 = tile(i, carry, diag_mask)
        acc_ref[0:64] = acc_ref[0:64] * (1.0 / l1)
        acc_ref[64:128] = acc_ref[64:128] * (1.0 / l2)
        o_ref[:, LANES * hp:LANES * (hp + 1)] = acc_ref[...].T.astype(BF16)


def _attn_b(bq, bk, bvt):
    n_q = SEQ // TQ
    return pl.pallas_call(
        _attn_b_kernel, grid=(BATCH, n_q),
        in_specs=[pl.BlockSpec((None, TQ, 1024), lambda b, i: (b, i, 0)),
                  pl.BlockSpec((None, SEQ, 1024), lambda b, i: (b, 0, 0)),
                  pl.BlockSpec((None, SEQ // TK, 512, TK), lambda b, i: (b, 0, 0, 0))],
        out_specs=pl.BlockSpec((None, TQ, 512), lambda b, i: (b, i, 0)),
        out_shape=jax.ShapeDtypeStruct((BATCH, SEQ, 512), BF16),
        scratch_shapes=[pltpu.VMEM((2 * B_V_DIM, TQ), F32)],
        compiler_params=_params(2), name="attn_b")(bq, bk, bvt)


def _stick_tile(k, q, vt, u_ref, mask, tail, acc_ref, rows):
    z = lax.dot_general(k, q, NT_DIMS, preferred_element_type=F32)
    log1pe = jnp.log(1.0 + jnp.exp(-jnp.abs(z)))
    sp = jnp.maximum(z, 0.0) + log1pe
    if mask is not None:
        sp = jnp.where(mask, sp, 0.0)
    tl = jnp.dot(u_ref[...], sp.astype(BF16), preferred_element_type=F32) + tail
    w = jnp.exp(jnp.minimum(z, 0.0) - log1pe - tl)
    if mask is not None:
        w = jnp.where(mask, w, 0.0)
    acc_ref[rows] = acc_ref[rows] + jnp.dot(vt, w.astype(BF16), preferred_element_type=F32)
    return tail + jnp.sum(sp, axis=0, keepdims=True)


def _attn_c_kernel(q_ref, k_ref, vt_ref, u_ref, o_ref, acc_ref):
    i = pl.program_id(1)
    r = lax.broadcasted_iota(jnp.int32, (TK, TQ), 0)
    c = lax.broadcasted_iota(jnp.int32, (TK, TQ), 1)
    diag_mask = r < c
    zero = jnp.zeros((1, TQ), F32)
    for hp in range(C_HEADS // 2):
        cols = slice(LANES * hp, LANES * (hp + 1))
        qs = _split_halves(q_ref[:, cols])
        acc_ref[...] = jnp.zeros_like(acc_ref)

        def tile(j, tails, mask):
            k = k_ref[pl.ds(pl.multiple_of(j * TK, TK), TK), cols]
            out = []
            for e in range(2):
                h = 2 * hp + e
                vt = vt_ref[j, C_HEAD_DIM * h:C_HEAD_DIM * (h + 1), :]
                out.append(_stick_tile(k, qs[e], vt, u_ref, mask, tails[e], acc_ref,
                                       slice(C_HEAD_DIM * e, C_HEAD_DIM * (e + 1))))
            return tuple(out)

        tails = tile(i, (zero, zero), diag_mask)
        lax.fori_loop(0, i, lambda t, tl: tile(i - 1 - t, tl, None), tails)
        o_ref[:, cols] = acc_ref[...].T.astype(BF16)


def _attn_c(q, k, vt, u):
    n_q = SEQ // TQ
    return pl.pallas_call(
        _attn_c_kernel, grid=(BATCH, n_q),
        in_specs=[pl.BlockSpec((None, TQ, C_WIDTH), lambda b, i: (b, i, 0)),
                  pl.BlockSpec((None, SEQ, C_WIDTH), lambda b, i: (b, 0, 0)),
                  pl.BlockSpec((None, SEQ // TK, C_WIDTH, TK), lambda b, i: (b, 0, 0, 0)),
                  _resident(u.shape)],
        out_specs=pl.BlockSpec((None, TQ, C_WIDTH), lambda b, i: (b, i, 0)),
        out_shape=jax.ShapeDtypeStruct((BATCH, SEQ, C_WIDTH), BF16),
        scratch_shapes=[pltpu.VMEM((2 * C_HEAD_DIM, TQ), F32)],
        compiler_params=_params(2), name="attn_c")(q, k, vt, u)


def _outproj_kernel(*refs):
    x_ref, part_refs, w_ref, o_ref = refs[0], refs[1:-2], refs[-2], refs[-1]
    acc = x_ref[...]
    off = 0
    for p_ref in part_refs:
        n = p_ref.shape[1]
        acc = acc + jnp.dot(p_ref[...], w_ref[off:off + n, :], preferred_element_type=F32)
        off += n
    o_ref[...] = acc


def _outproj(x2, parts, w):
    t = x2.shape[0]
    row = lambda n: pl.BlockSpec((TM_FFN, n), lambda i: (i, 0))
    return pl.pallas_call(
        _outproj_kernel, grid=(t // TM_FFN,),
        in_specs=[row(D_MODEL)] + [row(p.shape[1]) for p in parts] + [_resident(w.shape)],
        out_specs=row(D_MODEL), out_shape=jax.ShapeDtypeStruct(x2.shape, F32),
        compiler_params=_params(1), name="outproj")(x2, *parts, w)


def _ffn_kernel(x_ref, g_ref, wg_ref, wu_ref, wd_ref, o_ref, a_ref):
    x = x_ref[...]
    hb = _rms(x, g_ref[...]).astype(BF16)
    for c in range(D_FF // FF_CHUNK):
        sl = slice(c * FF_CHUNK, (c + 1) * FF_CHUNK)
        g = jnp.dot(hb, wg_ref[:, sl], preferred_element_type=F32)
        u = jnp.dot(hb, wu_ref[:, sl], preferred_element_type=F32)
        a_ref[:, sl] = (g * (1.0 / (1.0 + jnp.exp(-g))) * u).astype(BF16)
    o_ref[...] = x + jnp.dot(a_ref[...], wd_ref[...], preferred_element_type=F32)


def _ffn(x2, gain, wg, wu, wd):
    t = x2.shape[0]
    row = pl.BlockSpec((TM_FFN, D_MODEL), lambda i: (i, 0))
    return pl.pallas_call(
        _ffn_kernel, grid=(t // TM_FFN,),
        in_specs=[row, _resident(gain.shape), _resident(wg.shape), _resident(wu.shape),
                  _resident(wd.shape)],
        out_specs=row, out_shape=jax.ShapeDtypeStruct(x2.shape, F32),
        scratch_shapes=[pltpu.VMEM((TM_FFN, D_FF), BF16)],
        compiler_params=_params(1), name="ffn")(x2, gain, wg, wu, wd)


def _consts():
    lane = np.arange(LANES)
    pos = jnp.arange(SEQ, dtype=F32)[:, None]

    def table(half, idx, active, neg):
        inv_freq = jnp.power(ROPE_THETA, -jnp.arange(half, dtype=F32) / half)
        ang = pos * inv_freq[idx][None, :]
        act = jnp.asarray(active)[None, :]
        sign = jnp.asarray(np.where(neg, -1.0, 1.0), F32)[None, :]
        return jnp.where(act, jnp.cos(ang), 1.0), jnp.where(act, jnp.sin(ang) * sign, 0.0)

    cosa, sina = table(32, lane % 32, np.ones(LANES, bool), (lane % 64) < 32)
    rope_b = (lane >= 64) & (lane < 96)
    cosb, sinb = table(16, (lane - 64) % 16, rope_b, lane < 80)

    i256 = np.arange(MXU_DIM)
    ga = ((i256[:, None] // 64) == (i256[None, :] // 64)) / 64.0
    blk = i256 // LANES
    w = i256 % LANES
    grp = np.where(w < 64, 0, np.where(w < 96, 1, 2))
    same = (blk[:, None] == blk[None, :]) & (grp[:, None] == grp[None, :])
    scale = np.where(grp == 0, 1.0 / 64, np.where(grp == 1, 1.0 / 32, 0.0))
    gb = same * scale[None, :]
    it = np.arange(TK)
    u = (it[None, :] > it[:, None]).astype(np.float32)
    return dict(cosa=cosa, sina=sina, cosb=cosb, sinb=sinb,
                ga=jnp.asarray(ga, BF16), gb=jnp.asarray(gb, BF16), u=jnp.asarray(u, BF16))


def _pad_heads(wm, n_heads, width):
    k = wm.shape[0]
    w3 = wm.reshape(k, n_heads, width)
    return jnp.pad(w3, ((0, 0), (0, 0), (0, LANES - width))).reshape(k, n_heads * LANES)


def _tile_gain(parts, reps, scale=1.0):
    g = jnp.concatenate([p.astype(F32) for p in parts])
    g = jnp.pad(g, (0, LANES - g.shape[0]))
    return (jnp.tile(g, reps) * scale)[None, :]


def _even_weights(i, l, norm_mix, ab_w_in, a_q_norm, a_k_norm, a_out_norm, b_q_a_norm, b_w_q_b,
                  b_kv_a_norm, b_w_kv_b, b_q_nope_norm, b_q_rope_norm, b_k_nope_norm,
                  b_k_rope_norm):
    w_in = ab_w_in[i]
    kr_cols = jnp.pad(w_in[:, 1920:1952], ((0, 0), (64, 32)))
    wtok = jnp.concatenate([w_in[:, 0:1024], w_in[:, 1536:1920], kr_cols], axis=1).astype(BF16)
    wkv = b_w_kv_b[i].reshape(B_KV_RANK, B_HEADS, B_NOPE_DIM + B_V_DIM)
    sa = (A_QK_DIM ** -0.5) * LOG2E
    sb = ((B_NOPE_DIM + B_ROPE_DIM) ** -0.5) * LOG2E
    zeros32 = jnp.zeros((32,), F32)
    return dict(
        gmix=norm_mix[l][None, :],
        wtok=wtok,
        wavt=w_in[:, 1024:1536].T.astype(BF16),
        wqb=_pad_heads(b_w_q_b[i], B_HEADS, B_NOPE_DIM + B_ROPE_DIM).astype(BF16),
        wkb=_pad_heads(wkv[:, :, :B_NOPE_DIM].reshape(B_KV_RANK, -1), B_HEADS,
                       B_NOPE_DIM).astype(BF16),
        wbvt=wkv[:, :, B_NOPE_DIM:].reshape(B_KV_RANK, -1).T.astype(BF16),
        gaq=_tile_gain([a_q_norm[i], a_q_norm[i]], 4, sa),
        gak=_tile_gain([a_k_norm[i], a_k_norm[i]], 4),
        gqa=b_q_a_norm[i][None, :],
        gkva=b_kv_a_norm[i][None, :],
        gbq=_tile_gain([b_q_nope_norm[i], b_q_rope_norm[i]], 8, sb),
        gbk=_tile_gain([b_k_nope_norm[i]], 8),
        gkr=_tile_gain([jnp.zeros((64,), F32), b_k_rope_norm[i], zeros32], 1),
        gout=a_out_norm[i][None, :] * (1.0 - _lambda_init(l)),
    )


def kernel(x, norm_mix, norm_ffn, ab_w_in, a_q_norm, a_k_norm, a_lambda, a_out_norm, b_q_a_norm,
           b_w_q_b, b_kv_a_norm, b_w_kv_b, b_q_nope_norm, b_q_rope_norm, b_k_nope_norm,
           b_k_rope_norm, ab_w_out, c_w_in, c_w_out, ffn_w_gate, ffn_w_up, ffn_w_down):
    consts = _consts()
    t = BATCH * SEQ
    for l in range(DEPTH):
        i = l // 2
        x3 = x.reshape(BATCH, SEQ, D_MODEL)
        if l % 2 == 0:
            w = _even_weights(i, l, norm_mix, ab_w_in, a_q_norm, a_k_norm, a_out_norm, b_q_a_norm,
                              b_w_q_b, b_kv_a_norm, b_w_kv_b, b_q_nope_norm, b_q_rope_norm,
                              b_k_nope_norm, b_k_rope_norm)
            aq, ak, avt, bq, bk, bvt = _proj_even(x3, consts, w)
            out_a = _attn_a(aq, ak, avt, a_lambda[i], w["gout"], _lambda_init(l))
            out_b = _attn_b(bq, bk, bvt)
            parts = [out_a.reshape(t, 512), out_b.reshape(t, 512)]
            w_out = ab_w_out[i].astype(BF16)
        else:
            w_in = c_w_in[i]
            q, k, vt = _proj_odd(x3, norm_mix[l][None, :], w_in[:, :2 * C_WIDTH].astype(BF16),
                                 w_in[:, 2 * C_WIDTH:].T.astype(BF16))
            out_c = _attn_c(q, k, vt, consts["u"])
            parts = [out_c.reshape(t, C_WIDTH)]
            w_out = c_w_out[i].astype(BF16)
        x2 = _outproj(x.reshape(t, D_MODEL), parts, w_out)
        x2 = _ffn(x2, norm_ffn[l][None, :], ffn_w_gate[l].astype(BF16), ffn_w_up[l].astype(BF16),
                  ffn_w_down[l].astype(BF16))
        x = x2.reshape(BATCH, SEQ, D_MODEL)
    return x
```

```python
import functools
import math

import jax
import jax.numpy as jnp
import numpy as np
from jax import lax
from jax.experimental import pallas as pl
from jax.experimental.pallas import tpu as pltpu

D_MODEL = 1024
BATCH = 16
SEQ = 2048
DEPTH = 4
CHUNK = 64
ROPE_THETA = 10000.0
NORM_EPS = 1e-6

A_HEADS = 4
A_QK_DIM = 64
A_V_DIM = 128
B_HEADS = 8
B_Q_RANK = 256
B_KV_RANK = 128
B_NOPE_DIM = 64
B_ROPE_DIM = 32
B_V_DIM = 64
C_HEADS = 16
C_HEAD_DIM = 64
C_WIDTH = C_HEADS * C_HEAD_DIM
D_FF = 2816

LANES = 128
MXU_DIM = 256
TQ = 512
TK = 256
TM = 256
TM_FFN = 512
FF_CHUNK = 256
VMEM_LIMIT_BYTES = 56 * 1024 * 1024
LOG2E = 1.4426950408889634
NEG_BIG = -1e30
STICK_LAG = 1

F32 = jnp.float32
BF16 = jnp.bfloat16
NT_DIMS = (((1,), (1,)), ((), ()))


def _lambda_init(layer_idx):
    return 0.8 - 0.6 * math.exp(-0.3 * layer_idx)


def _params(n_axes):
    return pltpu.CompilerParams(dimension_semantics=("arbitrary",) * n_axes,
                                vmem_limit_bytes=VMEM_LIMIT_BYTES)


def _resident(shape):
    zeros = (0,) * len(shape)
    return pl.BlockSpec(shape, lambda *_: zeros, pipeline_mode=pl.Buffered(1))


def _rms(x, gain):
    ms = jnp.mean(x * x, axis=1, keepdims=True)
    return x * lax.rsqrt(ms + NORM_EPS) * gain


def _group_mean_sq(y, g_ref):
    outs = []
    for c in range(y.shape[1] // MXU_DIM):
        yc = y[:, c * MXU_DIM:(c + 1) * MXU_DIM]
        outs.append(jnp.dot((yc * yc).astype(BF16), g_ref[...], preferred_element_type=F32))
    return jnp.concatenate(outs, axis=1)


def _group_rms(y, g_ref, gain):
    return y * lax.rsqrt(_group_mean_sq(y, g_ref) + NORM_EPS) * gain


def _rope(y, first_half, half, cos, sin_signed):
    n = y.shape[1]
    partner = jnp.where(first_half, pltpu.roll(y, n - half, 1), pltpu.roll(y, half, 1))
    return y * cos + partner * sin_signed


def _split_halves(q):
    lo_lanes = lax.broadcasted_iota(jnp.int32, (q.shape[0], LANES), 1) < 64
    blocks = []
    for b in range(q.shape[1] // LANES):
        blk = q[:, b * LANES:(b + 1) * LANES]
        blocks += [jnp.where(lo_lanes, blk, 0.0), jnp.where(lo_lanes, 0.0, blk)]
    return jnp.concatenate(blocks, axis=1)


def _proj_even_kernel(x_ref, gmix_ref, wtok_ref, wavt_ref, wqb_ref, wkb_ref, wbvt_ref,
                      ga_ref, gb_ref, gaq_ref, gak_ref, gqa_ref, gkva_ref, gbq_ref, gbk_ref,
                      gkr_ref, cosa_ref, sina_ref, cosb_ref, sinb_ref,
                      aq_ref, ak_ref, avt_ref, bq_ref, bk_ref, bvt_ref):
    tm = x_ref.shape[0]
    hb = _rms(x_ref[...], gmix_ref[...]).astype(BF16)
    p = jnp.dot(hb, wtok_ref[...], preferred_element_type=F32)

    lane_a = lax.broadcasted_iota(jnp.int32, (tm, 512), 1)
    first_a = (lane_a & 32) == 0
    cosa = jnp.concatenate([cosa_ref[...]] * 4, axis=1)
    sina = jnp.concatenate([sina_ref[...]] * 4, axis=1)
    aq = _rope(_group_rms(p[:, 0:512], ga_ref, gaq_ref[...]), first_a, 32, cosa, sina)
    aq_ref[...] = _split_halves(aq).astype(BF16)
    ak = _rope(_group_rms(p[:, 512:1024], ga_ref, gak_ref[...]), first_a, 32, cosa, sina)
    ak_ref[...] = ak.astype(BF16)
    avt = lax.dot_general(wavt_ref[...], hb, NT_DIMS, preferred_element_type=F32)
    for c in range(tm // TK):
        avt_ref[c] = avt[:, c * TK:(c + 1) * TK].astype(BF16)

    qln = _rms(p[:, 1024:1280], gqa_ref[...]).astype(BF16)
    qb = jnp.dot(qln, wqb_ref[...], preferred_element_type=F32)
    lane_b = lax.broadcasted_iota(jnp.int32, (tm, 1024), 1)
    first_b = (lane_b & 127) < 80
    cosb = jnp.concatenate([cosb_ref[...]] * 8, axis=1)
    sinb = jnp.concatenate([sinb_ref[...]] * 8, axis=1)
    bq = _rope(_group_rms(qb, gb_ref, gbq_ref[...]), first_b, 16, cosb, sinb)
    bq_ref[...] = bq.astype(BF16)

    kvn = _rms(p[:, 1280:1408], gkva_ref[...]).astype(BF16)
    kb = _group_rms(jnp.dot(kvn, wkb_ref[...], preferred_element_type=F32),
                    gb_ref, gbk_ref[...])
    kr = p[:, 1408:1536]
    krn = kr * lax.rsqrt(jnp.sum(kr * kr, axis=1, keepdims=True) * (1.0 / B_ROPE_DIM)
                         + NORM_EPS) * gkr_ref[...]
    lane_r = lax.broadcasted_iota(jnp.int32, (tm, 128), 1)
    krr = _rope(krn, lane_r < 80, 16, cosb_ref[...], sinb_ref[...])
    bk_ref[...] = (kb + jnp.concatenate([krr] * 8, axis=1)).astype(BF16)
    bvt = lax.dot_general(wbvt_ref[...], kvn, NT_DIMS, preferred_element_type=F32)
    for c in range(tm // TK):
        bvt_ref[c] = bvt[:, c * TK:(c + 1) * TK].astype(BF16)


def _proj_even(x3, consts, w):
    n_s = SEQ // TM
    grid = (BATCH * n_s,)
    row = lambda n: pl.BlockSpec((None, TM, n), lambda i: (i // n_s, i % n_s, 0))
    vt = lambda n: pl.BlockSpec((None, TM // TK, n, TK), lambda i: (i // n_s, i % n_s, 0, 0))
    tab = pl.BlockSpec((TM, LANES), lambda i: (i % n_s, 0))
    ins = [x3, w["gmix"], w["wtok"], w["wavt"], w["wqb"], w["wkb"], w["wbvt"],
           consts["ga"], consts["gb"], w["gaq"], w["gak"], w["gqa"], w["gkva"], w["gbq"],
           w["gbk"], w["gkr"], consts["cosa"], consts["sina"], consts["cosb"], consts["sinb"]]
    in_specs = [row(D_MODEL)] + [_resident(a.shape) for a in ins[1:16]] + [tab] * 4
    out_shape = [
        jax.ShapeDtypeStruct((BATCH, SEQ, 1024), BF16),
        jax.ShapeDtypeStruct((BATCH, SEQ, 512), BF16),
        jax.ShapeDtypeStruct((BATCH, SEQ // TK, 512, TK), BF16),
        jax.ShapeDtypeStruct((BATCH, SEQ, 1024), BF16),
        jax.ShapeDtypeStruct((BATCH, SEQ, 1024), BF16),
        jax.ShapeDtypeStruct((BATCH, SEQ // TK, 512, TK), BF16),
    ]
    out_specs = [row(1024), row(512), vt(512), row(1024), row(1024), vt(512)]
    return pl.pallas_call(_proj_even_kernel, grid=grid, in_specs=in_specs, out_specs=out_specs,
                          out_shape=out_shape, compiler_params=_params(1),
                          name="proj_even")(*ins)


def _proj_odd_kernel(x_ref, gmix_ref, wqk_ref, wvt_ref, q_ref, k_ref, vt_ref):
    tm = x_ref.shape[0]
    hb = _rms(x_ref[...], gmix_ref[...]).astype(BF16)
    qk = jnp.dot(hb, wqk_ref[...], preferred_element_type=F32)
    q_ref[...] = _split_halves(qk[:, :C_WIDTH] * (C_HEAD_DIM ** -0.5 * LOG2E)).astype(BF16)
    k_ref[...] = qk[:, C_WIDTH:].astype(BF16)
    vt = lax.dot_general(wvt_ref[...], hb, NT_DIMS, preferred_element_type=F32)
    for c in range(tm // TK):
        vt_ref[c] = vt[:, c * TK:(c + 1) * TK].astype(BF16)


def _proj_odd(x3, gmix, wqk, wvt):
    n_s = SEQ // TM
    row = lambda n: pl.BlockSpec((None, TM, n), lambda i: (i // n_s, i % n_s, 0))
    vt = pl.BlockSpec((None, TM // TK, C_WIDTH, TK), lambda i: (i // n_s, i % n_s, 0, 0))
    out_shape = [
        jax.ShapeDtypeStruct((BATCH, SEQ, 2 * C_WIDTH), BF16),
        jax.ShapeDtypeStruct((BATCH, SEQ, C_WIDTH), BF16),
        jax.ShapeDtypeStruct((BATCH, SEQ // TK, C_WIDTH, TK), BF16),
    ]
    return pl.pallas_call(
        _proj_odd_kernel, grid=(BATCH * n_s,),
        in_specs=[row(D_MODEL), _resident(gmix.shape), _resident(wqk.shape), _resident(wvt.shape)],
        out_specs=[row(2 * C_WIDTH), row(C_WIDTH), vt], out_shape=out_shape,
        compiler_params=_params(1), name="proj_odd")(x3, gmix, wqk, wvt)


def _diag_masks(fn):
    r = lax.broadcasted_iota(jnp.int32, (TK, TQ), 0)
    c = lax.broadcasted_iota(jnp.int32, (TK, TQ), 1)
    return fn(r, c), fn(r + TK, c)


def _softmax_rest(s, vt, mask, c, m_ref, l_ref, acc_ref, rows):
    if mask is not None:
        s = jnp.where(mask, s, -jnp.inf)
    row = slice(8 * c, 8 * c + 1)
    m = m_ref[row, :]
    m_new = jnp.maximum(m, jnp.max(s, axis=0, keepdims=True))
    alpha = jnp.exp2(m - m_new)
    p = jnp.exp2(s - m_new)
    l_ref[row, :] = alpha * l_ref[row, :] + jnp.sum(p, axis=0, keepdims=True)
    m_ref[row, :] = m_new
    acc_ref[rows] = alpha * acc_ref[rows] + jnp.dot(vt, p.astype(BF16), preferred_element_type=F32)


def _softmax_attention(i, n, scores, rest, s_ref, m_ref, l_ref, acc_ref):
    acc_ref[...] = jnp.zeros_like(acc_ref)
    m_ref[...] = jnp.full_like(m_ref, NEG_BIG)
    l_ref[...] = jnp.zeros_like(l_ref)

    def phase(j, slot, mask, prefetch):
        for c in range(n):
            if prefetch:
                s_ref[1 - slot, c] = scores(j + 1, c)
            rest(j, c, s_ref[slot, c], mask)

    for c in range(n):
        s_ref[0, c] = scores(0, c)

    def body(u, carry):
        phase(2 * u, 0, None, True)
        phase(2 * u + 1, 1, None, True)
        return carry

    lax.fori_loop(0, i, body, 0)
    mask0, mask1 = _diag_masks(lambda kp, qp: (kp >> 6) <= (qp >> 6))
    phase(2 * i, 0, mask0, True)
    phase(2 * i + 1, 1, mask1, False)


def _attn_a_kernel(q_ref, k_ref, vt_ref, lam_ref, gout_ref, o_ref, s_ref, acc_ref, m_ref, l_ref,
                   *, lam0):
    i = pl.program_id(1)

    def scores(j, c):
        h = c // 2
        k = k_ref[pl.ds(pl.multiple_of(j * TK, TK), TK), LANES * h:LANES * (h + 1)]
        return lax.dot_general(k, q_ref[:, LANES * c:LANES * (c + 1)], NT_DIMS,
                               preferred_element_type=F32)

    def rest(j, c, s, mask):
        h = c // 2
        _softmax_rest(s, vt_ref[j, LANES * h:LANES * (h + 1), :], mask, c, m_ref, l_ref,
                      acc_ref, slice(LANES * c, LANES * (c + 1)))

    _softmax_attention(i, 2 * A_HEADS, scores, rest, s_ref, m_ref, l_ref, acc_ref)
    al = lam_ref[...]
    lam = (jnp.exp(jnp.sum(al[0:1] * al[1:2], axis=1, keepdims=True))
           - jnp.exp(jnp.sum(al[2:3] * al[3:4], axis=1, keepdims=True)) + lam0)
    for h in range(A_HEADS):
        c1, c2 = 2 * h, 2 * h + 1
        o1 = acc_ref[LANES * c1:LANES * (c1 + 1)] * (1.0 / l_ref[8 * c1:8 * c1 + 1, :])
        o2 = acc_ref[LANES * c2:LANES * (c2 + 1)] * (1.0 / l_ref[8 * c2:8 * c2 + 1, :])
        o = o1 - lam * o2
        o_ref[:, LANES * h:LANES * (h + 1)] = _rms(o.T, gout_ref[...]).astype(BF16)


def _kv_spec(n_lanes):
    return pl.BlockSpec((None, SEQ, n_lanes), lambda b, i: (b, 0, 0), pipeline_mode=pl.Buffered(1))


def _vt_spec(n_rows):
    return pl.BlockSpec((None, SEQ // TK, n_rows, TK), lambda b, i: (b, 0, 0, 0),
                        pipeline_mode=pl.Buffered(1))


def _attn_a(aq, ak, avt, a_lambda, gout, lam0):
    n_q = SEQ // TQ
    n_chain = 2 * A_HEADS
    kern = functools.partial(_attn_a_kernel, lam0=lam0)
    return pl.pallas_call(
        kern, grid=(BATCH, n_q),
        in_specs=[pl.BlockSpec((None, TQ, n_chain * LANES), lambda b, i: (b, i, 0)),
                  _kv_spec(512), _vt_spec(512),
                  _resident(a_lambda.shape), _resident(gout.shape)],
        out_specs=pl.BlockSpec((None, TQ, 512), lambda b, i: (b, i, 0)),
        out_shape=jax.ShapeDtypeStruct((BATCH, SEQ, 512), BF16),
        scratch_shapes=[pltpu.VMEM((2, n_chain, TK, TQ), F32),
                        pltpu.VMEM((n_chain * A_V_DIM, TQ), F32),
                        pltpu.VMEM((8 * n_chain, TQ), F32), pltpu.VMEM((8 * n_chain, TQ), F32)],
        compiler_params=_params(2), name="attn_a")(aq, ak, avt, a_lambda, gout)


def _attn_b_kernel(q_ref, k_ref, vt_ref, o_ref, s_ref, acc_ref, m_ref, l_ref):
    i = pl.program_id(1)

    def scores(j, h):
        cols = slice(LANES * h, LANES * (h + 1))
        return lax.dot_general(k_ref[pl.ds(pl.multiple_of(j * TK, TK), TK), cols], q_ref[:, cols],
                               NT_DIMS, preferred_element_type=F32)

    def rest(j, h, s, mask):
        rows = slice(B_V_DIM * h, B_V_DIM * (h + 1))
        _softmax_rest(s, vt_ref[j, rows, :], mask, h, m_ref, l_ref, acc_ref, rows)

    _softmax_attention(i, B_HEADS, scores, rest, s_ref, m_ref, l_ref, acc_ref)
    for hp in range(B_HEADS // 2):
        parts = []
        for h in (2 * hp, 2 * hp + 1):
            rows = slice(B_V_DIM * h, B_V_DIM * (h + 1))
            parts.append(acc_ref[rows] * (1.0 / l_ref[8 * h:8 * h + 1, :]))
        o_ref[:, LANES * hp:LANES * (hp + 1)] = jnp.concatenate(parts, axis=0).T.astype(BF16)


def _attn_b(bq, bk, bvt):
    n_q = SEQ // TQ
    return pl.pallas_call(
        _attn_b_kernel, grid=(BATCH, n_q),
        in_specs=[pl.BlockSpec((None, TQ, 1024), lambda b, i: (b, i, 0)),
                  _kv_spec(1024), _vt_spec(512)],
        out_specs=pl.BlockSpec((None, TQ, 512), lambda b, i: (b, i, 0)),
        out_shape=jax.ShapeDtypeStruct((BATCH, SEQ, 512), BF16),
        scratch_shapes=[pltpu.VMEM((2, B_HEADS, TK, TQ), F32),
                        pltpu.VMEM((B_HEADS * B_V_DIM, TQ), F32),
                        pltpu.VMEM((8 * B_HEADS, TQ), F32), pltpu.VMEM((8 * B_HEADS, TQ), F32)],
        compiler_params=_params(2), name="attn_b")(bq, bk, bvt)


def _stick_front(z, u_ref, mask):
    log1p2 = jnp.log(1.0 + jnp.exp2(jnp.minimum(z, 126.0))) * LOG2E
    sp = jnp.maximum(z, log1p2)
    logsig = z - sp
    if mask is not None:
        sp = jnp.where(mask, sp, 0.0)
    tl = jnp.dot(u_ref[...], sp.astype(BF16), preferred_element_type=F32)
    return logsig, tl[0:1, :] + sp[0:1, :], tl


def _stick_back(front, vt, mask, c, tail_ref, acc_ref, rows):
    logsig, tile_mass, tl = front
    row = slice(8 * c, 8 * c + 1)
    tail = tail_ref[row, :]
    w = jnp.exp2(logsig - tl)
    if mask is not None:
        w = jnp.where(mask, w, 0.0)
    pv = jnp.dot(vt, w.astype(BF16), preferred_element_type=F32)
    acc_ref[rows] = acc_ref[rows] + pv * jnp.exp2(-tail)
    tail_ref[row, :] = tail + tile_mass


def _attn_c_kernel(q_ref, k_ref, vt_ref, u_ref, o_ref, s_ref, acc_ref, tail_ref):
    i = pl.program_id(1)
    n = C_HEADS

    def scores(j, h):
        k = k_ref[pl.ds(pl.multiple_of(j * TK, TK), TK), LANES * (h // 2):LANES * (h // 2 + 1)]
        return lax.dot_general(k, q_ref[:, LANES * h:LANES * (h + 1)], NT_DIMS,
                               preferred_element_type=F32)

    def phase(j, slot, mask, j_next):
        fronts = {}
        for c in range(n + STICK_LAG):
            if c < n:
                s_ref[1 - slot, c] = scores(j_next, c)
                fronts[c] = _stick_front(s_ref[slot, c], u_ref, mask)
            d = c - STICK_LAG
            if d >= 0:
                rows = slice(C_HEAD_DIM * d, C_HEAD_DIM * (d + 1))
                _stick_back(fronts.pop(d), vt_ref[j, rows, :], mask, d, tail_ref, acc_ref, rows)

    acc_ref[...] = jnp.zeros_like(acc_ref)
    tail_ref[...] = jnp.zeros_like(tail_ref)
    for c in range(n):
        s_ref[0, c] = scores(2 * i + 1, c)
    mask0, mask1 = _diag_masks(lambda kp, qp: kp < qp)
    phase(2 * i + 1, 0, mask1, 2 * i)
    phase(2 * i, 1, mask0, jnp.maximum(2 * i - 1, 0))

    def body(u, carry):
        j = 2 * i - 1 - 2 * u
        phase(j, 0, None, j - 1)
        phase(j - 1, 1, None, jnp.maximum(j - 2, 0))
        return carry

    lax.fori_loop(0, i, body, 0)
    for hp in range(C_HEADS // 2):
        blk = slice(LANES * hp, LANES * (hp + 1))
        o_ref[:, blk] = acc_ref[blk].T.astype(BF16)


def _attn_c(q, k, vt, u):
    n_q = SEQ // TQ
    return pl.pallas_call(
        _attn_c_kernel, grid=(BATCH, n_q),
        in_specs=[pl.BlockSpec((None, TQ, 2 * C_WIDTH), lambda b, i: (b, i, 0)),
                  _kv_spec(C_WIDTH), _vt_spec(C_WIDTH), _resident(u.shape)],
        out_specs=pl.BlockSpec((None, TQ, C_WIDTH), lambda b, i: (b, i, 0)),
        out_shape=jax.ShapeDtypeStruct((BATCH, SEQ, C_WIDTH), BF16),
        scratch_shapes=[pltpu.VMEM((2, C_HEADS, TK, TQ), F32), pltpu.VMEM((C_WIDTH, TQ), F32),
                        pltpu.VMEM((8 * C_HEADS, TQ), F32)],
        compiler_params=_params(2), name="attn_c")(q, k, vt, u)


def _outproj_kernel(*refs):
    x_ref, part_refs, w_ref, o_ref = refs[0], refs[1:-2], refs[-2], refs[-1]
    acc = x_ref[...]
    off = 0
    for p_ref in part_refs:
        n = p_ref.shape[1]
        acc = acc + jnp.dot(p_ref[...], w_ref[off:off + n, :], preferred_element_type=F32)
        off += n
    o_ref[...] = acc


def _outproj(x2, parts, w):
    t = x2.shape[0]
    row = lambda n: pl.BlockSpec((TM_FFN, n), lambda i: (i, 0))
    return pl.pallas_call(
        _outproj_kernel, grid=(t // TM_FFN,),
        in_specs=[row(D_MODEL)] + [row(p.shape[1]) for p in parts] + [_resident(w.shape)],
        out_specs=row(D_MODEL), out_shape=jax.ShapeDtypeStruct(x2.shape, F32),
        compiler_params=_params(1), name="outproj")(x2, *parts, w)


def _ffn_kernel(x_ref, g_ref, wg_ref, wu_ref, wd_ref, o_ref, a_ref):
    x = x_ref[...]
    hb = _rms(x, g_ref[...]).astype(BF16)
    for c in range(D_FF // FF_CHUNK):
        sl = slice(c * FF_CHUNK, (c + 1) * FF_CHUNK)
        g = jnp.dot(hb, wg_ref[:, sl], preferred_element_type=F32)
        u = jnp.dot(hb, wu_ref[:, sl], preferred_element_type=F32)
        a_ref[:, sl] = (g * (1.0 / (1.0 + jnp.exp(-g))) * u).astype(BF16)
    o_ref[...] = x + jnp.dot(a_ref[...], wd_ref[...], preferred_element_type=F32)


def _ffn(x2, gain, wg, wu, wd):
    t = x2.shape[0]
    row = pl.BlockSpec((TM_FFN, D_MODEL), lambda i: (i, 0))
    return pl.pallas_call(
        _ffn_kernel, grid=(t // TM_FFN,),
        in_specs=[row, _resident(gain.shape), _resident(wg.shape), _resident(wu.shape),
                  _resident(wd.shape)],
        out_specs=row, out_shape=jax.ShapeDtypeStruct(x2.shape, F32),
        scratch_shapes=[pltpu.VMEM((TM_FFN, D_FF), BF16)],
        compiler_params=_params(1), name="ffn")(x2, gain, wg, wu, wd)


def _consts():
    lane = np.arange(LANES)
    pos = jnp.arange(SEQ, dtype=F32)[:, None]

    def table(half, idx, active, neg):
        inv_freq = jnp.power(ROPE_THETA, -jnp.arange(half, dtype=F32) / half)
        ang = pos * inv_freq[idx][None, :]
        act = jnp.asarray(active)[None, :]
        sign = jnp.asarray(np.where(neg, -1.0, 1.0), F32)[None, :]
        return jnp.where(act, jnp.cos(ang), 1.0), jnp.where(act, jnp.sin(ang) * sign, 0.0)

    cosa, sina = table(32, lane % 32, np.ones(LANES, bool), (lane % 64) < 32)
    rope_b = (lane >= 64) & (lane < 96)
    cosb, sinb = table(16, (lane - 64) % 16, rope_b, lane < 80)

    i256 = np.arange(MXU_DIM)
    ga = ((i256[:, None] // 64) == (i256[None, :] // 64)) / 64.0
    blk = i256 // LANES
    w = i256 % LANES
    grp = np.where(w < 64, 0, np.where(w < 96, 1, 2))
    same = (blk[:, None] == blk[None, :]) & (grp[:, None] == grp[None, :])
    scale = np.where(grp == 0, 1.0 / 64, np.where(grp == 1, 1.0 / 32, 0.0))
    gb = same * scale[None, :]
    it = np.arange(TK)
    u = (it[None, :] > it[:, None]).astype(np.float32)
    return dict(cosa=cosa, sina=sina, cosb=cosb, sinb=sinb,
                ga=jnp.asarray(ga, BF16), gb=jnp.asarray(gb, BF16), u=jnp.asarray(u, BF16))


def _pad_heads(wm, n_heads, width):
    k = wm.shape[0]
    w3 = wm.reshape(k, n_heads, width)
    return jnp.pad(w3, ((0, 0), (0, 0), (0, LANES - width))).reshape(k, n_heads * LANES)


def _tile_gain(parts, reps, scale=1.0):
    g = jnp.concatenate([p.astype(F32) for p in parts])
    g = jnp.pad(g, (0, LANES - g.shape[0]))
    return (jnp.tile(g, reps) * scale)[None, :]


def _even_weights(i, l, norm_mix, ab_w_in, a_q_norm, a_k_norm, a_out_norm, b_q_a_norm, b_w_q_b,
                  b_kv_a_norm, b_w_kv_b, b_q_nope_norm, b_q_rope_norm, b_k_nope_norm,
                  b_k_rope_norm):
    w_in = ab_w_in[i]
    kr_cols = jnp.pad(w_in[:, 1920:1952], ((0, 0), (64, 32)))
    wtok = jnp.concatenate([w_in[:, 0:1024], w_in[:, 1536:1920], kr_cols], axis=1).astype(BF16)
    wkv = b_w_kv_b[i].reshape(B_KV_RANK, B_HEADS, B_NOPE_DIM + B_V_DIM)
    sa = (A_QK_DIM ** -0.5) * LOG2E
    sb = ((B_NOPE_DIM + B_ROPE_DIM) ** -0.5) * LOG2E
    zeros32 = jnp.zeros((32,), F32)
    return dict(
        gmix=norm_mix[l][None, :],
        wtok=wtok,
        wavt=w_in[:, 1024:1536].T.astype(BF16),
        wqb=_pad_heads(b_w_q_b[i], B_HEADS, B_NOPE_DIM + B_ROPE_DIM).astype(BF16),
        wkb=_pad_heads(wkv[:, :, :B_NOPE_DIM].reshape(B_KV_RANK, -1), B_HEADS,
                       B_NOPE_DIM).astype(BF16),
        wbvt=wkv[:, :, B_NOPE_DIM:].reshape(B_KV_RANK, -1).T.astype(BF16),
        gaq=_tile_gain([a_q_norm[i], a_q_norm[i]], 4, sa),
        gak=_tile_gain([a_k_norm[i], a_k_norm[i]], 4),
        gqa=b_q_a_norm[i][None, :],
        gkva=b_kv_a_norm[i][None, :],
        gbq=_tile_gain([b_q_nope_norm[i], b_q_rope_norm[i]], 8, sb),
        gbk=_tile_gain([b_k_nope_norm[i]], 8),
        gkr=_tile_gain([jnp.zeros((64,), F32), b_k_rope_norm[i], zeros32], 1),
        gout=a_out_norm[i][None, :] * (1.0 - _lambda_init(l)),
    )


def kernel(x, norm_mix, norm_ffn, ab_w_in, a_q_norm, a_k_norm, a_lambda, a_out_norm, b_q_a_norm,
           b_w_q_b, b_kv_a_norm, b_w_kv_b, b_q_nope_norm, b_q_rope_norm, b_k_nope_norm,
           b_k_rope_norm, ab_w_out, c_w_in, c_w_out, ffn_w_gate, ffn_w_up, ffn_w_down):
    consts = _consts()
    t = BATCH * SEQ
    for l in range(DEPTH):
        i = l // 2
        x3 = x.reshape(BATCH, SEQ, D_MODEL)
        if l % 2 == 0:
            w = _even_weights(i, l, norm_mix, ab_w_in, a_q_norm, a_k_norm, a_out_norm, b_q_a_norm,
                              b_w_q_b, b_kv_a_norm, b_w_kv_b, b_q_nope_norm, b_q_rope_norm,
                              b_k_nope_norm, b_k_rope_norm)
            aq, ak, avt, bq, bk, bvt = _proj_even(x3, consts, w)
            out_a = _attn_a(aq, ak, avt, a_lambda[i], w["gout"], _lambda_init(l))
            out_b = _attn_b(bq, bk, bvt)
            parts = [out_a.reshape(t, 512), out_b.reshape(t, 512)]
            w_out = ab_w_out[i].astype(BF16)
        else:
            w_in = c_w_in[i]
            q, k, vt = _proj_odd(x3, norm_mix[l][None, :], w_in[:, :2 * C_WIDTH].astype(BF16),
                                 w_in[:, 2 * C_WIDTH:].T.astype(BF16))
            out_c = _attn_c(q, k, vt, consts["u"])
            parts = [out_c.reshape(t, C_WIDTH)]
            w_out = c_w_out[i].astype(BF16)
        x2 = _outproj(x.reshape(t, D_MODEL), parts, w_out)
        x2 = _ffn(x2, norm_ffn[l][None, :], ffn_w_gate[l].astype(BF16), ffn_w_up[l].astype(BF16),
                  ffn_w_down[l].astype(BF16))
        x = x2.reshape(BATCH, SEQ, D_MODEL)
    return x
```

```python
import functools
import math

import jax
import jax.numpy as jnp
import numpy as np
from jax import lax
from jax.experimental import pallas as pl
from jax.experimental.pallas import tpu as pltpu

D_MODEL = 1024
BATCH = 16
SEQ = 2048
DEPTH = 4
CHUNK = 64
ROPE_THETA = 10000.0
NORM_EPS = 1e-6

A_HEADS = 4
A_QK_DIM = 64
A_V_DIM = 128
B_HEADS = 8
B_Q_RANK = 256
B_KV_RANK = 128
B_NOPE_DIM = 64
B_ROPE_DIM = 32
B_V_DIM = 64
C_HEADS = 16
C_HEAD_DIM = 64
C_WIDTH = C_HEADS * C_HEAD_DIM
D_FF = 2816

LANES = 128
MXU_DIM = 256
TQ = 512
TK = 256
TM = 256
TM_FFN = 512
FF_CHUNK = 256
VMEM_LIMIT_BYTES = 56 * 1024 * 1024
LOG2E = 1.4426950408889634
NEG_BIG = -1e30
DEAD_TAIL = 160.0
STICK_LAG = 1

F32 = jnp.float32
BF16 = jnp.bfloat16
NT_DIMS = (((1,), (1,)), ((), ()))


def _lambda_init(layer_idx):
    return 0.8 - 0.6 * math.exp(-0.3 * layer_idx)


def _params(n_axes):
    return pltpu.CompilerParams(dimension_semantics=("arbitrary",) * n_axes,
                                vmem_limit_bytes=VMEM_LIMIT_BYTES)


def _resident(shape):
    zeros = (0,) * len(shape)
    return pl.BlockSpec(shape, lambda *_: zeros, pipeline_mode=pl.Buffered(1))


def _rms(x, gain):
    ms = jnp.mean(x * x, axis=1, keepdims=True)
    return x * lax.rsqrt(ms + NORM_EPS) * gain


def _group_mean_sq(y, g_ref):
    outs = []
    for c in range(y.shape[1] // MXU_DIM):
        yc = y[:, c * MXU_DIM:(c + 1) * MXU_DIM]
        outs.append(jnp.dot((yc * yc).astype(BF16), g_ref[...], preferred_element_type=F32))
    return jnp.concatenate(outs, axis=1)


def _group_rms(y, g_ref, gain):
    return y * lax.rsqrt(_group_mean_sq(y, g_ref) + NORM_EPS) * gain


def _rope(y, first_half, half, cos, sin_signed):
    n = y.shape[1]
    partner = jnp.where(first_half, pltpu.roll(y, n - half, 1), pltpu.roll(y, half, 1))
    return y * cos + partner * sin_signed


def _split_halves(q):
    lo_lanes = lax.broadcasted_iota(jnp.int32, (q.shape[0], LANES), 1) < 64
    blocks = []
    for b in range(q.shape[1] // LANES):
        blk = q[:, b * LANES:(b + 1) * LANES]
        blocks += [jnp.where(lo_lanes, blk, 0.0), jnp.where(lo_lanes, 0.0, blk)]
    return jnp.concatenate(blocks, axis=1)


def _proj_even_kernel(x_ref, gmix_ref, wtok_ref, wavt_ref, wqb_ref, wkb_ref, wbvt_ref,
                      ga_ref, gb_ref, gaq_ref, gak_ref, gqa_ref, gkva_ref, gbq_ref, gbk_ref,
                      gkr_ref, cosa_ref, sina_ref, cosb_ref, sinb_ref,
                      aq_ref, ak_ref, avt_ref, bq_ref, bk_ref, bvt_ref):
    tm = x_ref.shape[0]
    hb = _rms(x_ref[...], gmix_ref[...]).astype(BF16)
    p = jnp.dot(hb, wtok_ref[...], preferred_element_type=F32)

    lane_a = lax.broadcasted_iota(jnp.int32, (tm, 512), 1)
    first_a = (lane_a & 32) == 0
    cosa = jnp.concatenate([cosa_ref[...]] * 4, axis=1)
    sina = jnp.concatenate([sina_ref[...]] * 4, axis=1)
    aq = _rope(_group_rms(p[:, 0:512], ga_ref, gaq_ref[...]), first_a, 32, cosa, sina)
    aq_ref[...] = _split_halves(aq).astype(BF16)
    ak = _rope(_group_rms(p[:, 512:1024], ga_ref, gak_ref[...]), first_a, 32, cosa, sina)
    ak_ref[...] = ak.astype(BF16)
    avt = lax.dot_general(wavt_ref[...], hb, NT_DIMS, preferred_element_type=F32)
    for c in range(tm // TK):
        avt_ref[c] = avt[:, c * TK:(c + 1) * TK].astype(BF16)

    qln = _rms(p[:, 1024:1280], gqa_ref[...]).astype(BF16)
    qb = jnp.dot(qln, wqb_ref[...], preferred_element_type=F32)
    lane_b = lax.broadcasted_iota(jnp.int32, (tm, 1024), 1)
    first_b = (lane_b & 127) < 80
    cosb = jnp.concatenate([cosb_ref[...]] * 8, axis=1)
    sinb = jnp.concatenate([sinb_ref[...]] * 8, axis=1)
    bq = _rope(_group_rms(qb, gb_ref, gbq_ref[...]), first_b, 16, cosb, sinb)
    bq_ref[...] = bq.astype(BF16)

    kvn = _rms(p[:, 1280:1408], gkva_ref[...]).astype(BF16)
    kb = _group_rms(jnp.dot(kvn, wkb_ref[...], preferred_element_type=F32),
                    gb_ref, gbk_ref[...])
    kr = p[:, 1408:1536]
    krn = kr * lax.rsqrt(jnp.sum(kr * kr, axis=1, keepdims=True) * (1.0 / B_ROPE_DIM)
                         + NORM_EPS) * gkr_ref[...]
    lane_r = lax.broadcasted_iota(jnp.int32, (tm, 128), 1)
    krr = _rope(krn, lane_r < 80, 16, cosb_ref[...], sinb_ref[...])
    bk_ref[...] = (kb + jnp.concatenate([krr] * 8, axis=1)).astype(BF16)
    bvt = lax.dot_general(wbvt_ref[...], kvn, NT_DIMS, preferred_element_type=F32)
    for c in range(tm // TK):
        bvt_ref[c] = bvt[:, c * TK:(c + 1) * TK].astype(BF16)


def _proj_even(x3, consts, w):
    n_s = SEQ // TM
    grid = (BATCH * n_s,)
    row = lambda n: pl.BlockSpec((None, TM, n), lambda i: (i // n_s, i % n_s, 0))
    vt = lambda n: pl.BlockSpec((None, TM // TK, n, TK), lambda i: (i // n_s, i % n_s, 0, 0))
    tab = pl.BlockSpec((TM, LANES), lambda i: (i % n_s, 0))
    ins = [x3, w["gmix"], w["wtok"], w["wavt"], w["wqb"], w["wkb"], w["wbvt"],
           consts["ga"], consts["gb"], w["gaq"], w["gak"], w["gqa"], w["gkva"], w["gbq"],
           w["gbk"], w["gkr"], consts["cosa"], consts["sina"], consts["cosb"], consts["sinb"]]
    in_specs = [row(D_MODEL)] + [_resident(a.shape) for a in ins[1:16]] + [tab] * 4
    out_shape = [
        jax.ShapeDtypeStruct((BATCH, SEQ, 1024), BF16),
        jax.ShapeDtypeStruct((BATCH, SEQ, 512), BF16),
        jax.ShapeDtypeStruct((BATCH, SEQ // TK, 512, TK), BF16),
        jax.ShapeDtypeStruct((BATCH, SEQ, 1024), BF16),
        jax.ShapeDtypeStruct((BATCH, SEQ, 1024), BF16),
        jax.ShapeDtypeStruct((BATCH, SEQ // TK, 512, TK), BF16),
    ]
    out_specs = [row(1024), row(512), vt(512), row(1024), row(1024), vt(512)]
    return pl.pallas_call(_proj_even_kernel, grid=grid, in_specs=in_specs, out_specs=out_specs,
                          out_shape=out_shape, compiler_params=_params(1),
                          name="proj_even")(*ins)


def _proj_odd_kernel(x_ref, gmix_ref, wqk_ref, wvt_ref, q_ref, k_ref, vt_ref):
    tm = x_ref.shape[0]
    hb = _rms(x_ref[...], gmix_ref[...]).astype(BF16)
    qk = jnp.dot(hb, wqk_ref[...], preferred_element_type=F32)
    q_ref[...] = _split_halves(qk[:, :C_WIDTH] * (C_HEAD_DIM ** -0.5 * LOG2E)).astype(BF16)
    k_ref[...] = qk[:, C_WIDTH:].astype(BF16)
    vt = lax.dot_general(wvt_ref[...], hb, NT_DIMS, preferred_element_type=F32)
    for c in range(tm // TK):
        vt_ref[c] = vt[:, c * TK:(c + 1) * TK].astype(BF16)


def _proj_odd(x3, gmix, wqk, wvt):
    n_s = SEQ // TM
    row = lambda n: pl.BlockSpec((None, TM, n), lambda i: (i // n_s, i % n_s, 0))
    vt = pl.BlockSpec((None, TM // TK, C_WIDTH, TK), lambda i: (i // n_s, i % n_s, 0, 0))
    out_shape = [
        jax.ShapeDtypeStruct((BATCH, SEQ, 2 * C_WIDTH), BF16),
        jax.ShapeDtypeStruct((BATCH, SEQ, C_WIDTH), BF16),
        jax.ShapeDtypeStruct((BATCH, SEQ // TK, C_WIDTH, TK), BF16),
    ]
    return pl.pallas_call(
        _proj_odd_kernel, grid=(BATCH * n_s,),
        in_specs=[row(D_MODEL), _resident(gmix.shape), _resident(wqk.shape), _resident(wvt.shape)],
        out_specs=[row(2 * C_WIDTH), row(C_WIDTH), vt], out_shape=out_shape,
        compiler_params=_params(1), name="proj_odd")(x3, gmix, wqk, wvt)


def _diag_masks(fn):
    r = lax.broadcasted_iota(jnp.int32, (TK, TQ), 0)
    c = lax.broadcasted_iota(jnp.int32, (TK, TQ), 1)
    return fn(r, c), fn(r + TK, c)


def _softmax_rest(s, vt, mask, c, m_ref, l_ref, acc_ref, rows):
    if mask is not None:
        s = jnp.where(mask, s, -jnp.inf)
    row = slice(8 * c, 8 * c + 1)
    m = m_ref[row, :]
    m_new = jnp.maximum(m, jnp.max(s, axis=0, keepdims=True))
    alpha = jnp.exp2(m - m_new)
    p = jnp.exp2(s - m_new)
    l_ref[row, :] = alpha * l_ref[row, :] + jnp.sum(p, axis=0, keepdims=True)
    m_ref[row, :] = m_new
    acc_ref[rows] = alpha * acc_ref[rows] + jnp.dot(vt, p.astype(BF16), preferred_element_type=F32)


def _softmax_attention(i, n, scores, rest, s_ref, m_ref, l_ref, acc_ref):
    acc_ref[...] = jnp.zeros_like(acc_ref)
    m_ref[...] = jnp.full_like(m_ref, NEG_BIG)
    l_ref[...] = jnp.zeros_like(l_ref)

    def phase(j, slot, mask, prefetch):
        for c in range(n):
            if prefetch:
                s_ref[1 - slot, c] = scores(j + 1, c)
            rest(j, c, s_ref[slot, c], mask)

    for c in range(n):
        s_ref[0, c] = scores(0, c)

    def body(u, carry):
        phase(2 * u, 0, None, True)
        phase(2 * u + 1, 1, None, True)
        return carry

    lax.fori_loop(0, i, body, 0)
    mask0, mask1 = _diag_masks(lambda kp, qp: (kp >> 6) <= (qp >> 6))
    phase(2 * i, 0, mask0, True)
    phase(2 * i + 1, 1, mask1, False)


def _attn_a_kernel(q_ref, k_ref, vt_ref, lam_ref, gout_ref, o_ref, s_ref, acc_ref, m_ref, l_ref,
                   *, lam0):
    i = pl.program_id(1)

    def scores(j, c):
        h = c // 2
        k = k_ref[pl.ds(pl.multiple_of(j * TK, TK), TK), LANES * h:LANES * (h + 1)]
        return lax.dot_general(k, q_ref[:, LANES * c:LANES * (c + 1)], NT_DIMS,
                               preferred_element_type=F32)

    def rest(j, c, s, mask):
        h = c // 2
        _softmax_rest(s, vt_ref[j, LANES * h:LANES * (h + 1), :], mask, c, m_ref, l_ref,
                      acc_ref, slice(LANES * c, LANES * (c + 1)))

    _softmax_attention(i, 2 * A_HEADS, scores, rest, s_ref, m_ref, l_ref, acc_ref)
    al = lam_ref[...]
    lam = (jnp.exp(jnp.sum(al[0:1] * al[1:2], axis=1, keepdims=True))
           - jnp.exp(jnp.sum(al[2:3] * al[3:4], axis=1, keepdims=True)) + lam0)
    for h in range(A_HEADS):
        c1, c2 = 2 * h, 2 * h + 1
        o1 = acc_ref[LANES * c1:LANES * (c1 + 1)] * (1.0 / l_ref[8 * c1:8 * c1 + 1, :])
        o2 = acc_ref[LANES * c2:LANES * (c2 + 1)] * (1.0 / l_ref[8 * c2:8 * c2 + 1, :])
        o = o1 - lam * o2
        o_ref[:, LANES * h:LANES * (h + 1)] = _rms(o.T, gout_ref[...]).astype(BF16)


def _kv_spec(n_lanes):
    return pl.BlockSpec((None, SEQ, n_lanes), lambda b, i: (b, 0, 0), pipeline_mode=pl.Buffered(1))


def _vt_spec(n_rows):
    return pl.BlockSpec((None, SEQ // TK, n_rows, TK), lambda b, i: (b, 0, 0, 0),
                        pipeline_mode=pl.Buffered(1))


def _attn_a(aq, ak, avt, a_lambda, gout, lam0):
    n_q = SEQ // TQ
    n_chain = 2 * A_HEADS
    kern = functools.partial(_attn_a_kernel, lam0=lam0)
    return pl.pallas_call(
        kern, grid=(BATCH, n_q),
        in_specs=[pl.BlockSpec((None, TQ, n_chain * LANES), lambda b, i: (b, i, 0)),
                  _kv_spec(512), _vt_spec(512),
                  _resident(a_lambda.shape), _resident(gout.shape)],
        out_specs=pl.BlockSpec((None, TQ, 512), lambda b, i: (b, i, 0)),
        out_shape=jax.ShapeDtypeStruct((BATCH, SEQ, 512), BF16),
        scratch_shapes=[pltpu.VMEM((2, n_chain, TK, TQ), F32),
                        pltpu.VMEM((n_chain * A_V_DIM, TQ), F32),
                        pltpu.VMEM((8 * n_chain, TQ), F32), pltpu.VMEM((8 * n_chain, TQ), F32)],
        compiler_params=_params(2), name="attn_a")(aq, ak, avt, a_lambda, gout)


def _attn_b_kernel(q_ref, k_ref, vt_ref, o_ref, s_ref, acc_ref, m_ref, l_ref):
    i = pl.program_id(1)

    def scores(j, h):
        cols = slice(LANES * h, LANES * (h + 1))
        return lax.dot_general(k_ref[pl.ds(pl.multiple_of(j * TK, TK), TK), cols], q_ref[:, cols],
                               NT_DIMS, preferred_element_type=F32)

    def rest(j, h, s, mask):
        rows = slice(B_V_DIM * h, B_V_DIM * (h + 1))
        _softmax_rest(s, vt_ref[j, rows, :], mask, h, m_ref, l_ref, acc_ref, rows)

    _softmax_attention(i, B_HEADS, scores, rest, s_ref, m_ref, l_ref, acc_ref)
    for hp in range(B_HEADS // 2):
        parts = []
        for h in (2 * hp, 2 * hp + 1):
            rows = slice(B_V_DIM * h, B_V_DIM * (h + 1))
            parts.append(acc_ref[rows] * (1.0 / l_ref[8 * h:8 * h + 1, :]))
        o_ref[:, LANES * hp:LANES * (hp + 1)] = jnp.concatenate(parts, axis=0).T.astype(BF16)


def _attn_b(bq, bk, bvt):
    n_q = SEQ // TQ
    return pl.pallas_call(
        _attn_b_kernel, grid=(BATCH, n_q),
        in_specs=[pl.BlockSpec((None, TQ, 1024), lambda b, i: (b, i, 0)),
                  _kv_spec(1024), _vt_spec(512)],
        out_specs=pl.BlockSpec((None, TQ, 512), lambda b, i: (b, i, 0)),
        out_shape=jax.ShapeDtypeStruct((BATCH, SEQ, 512), BF16),
        scratch_shapes=[pltpu.VMEM((2, B_HEADS, TK, TQ), F32),
                        pltpu.VMEM((B_HEADS * B_V_DIM, TQ), F32),
                        pltpu.VMEM((8 * B_HEADS, TQ), F32), pltpu.VMEM((8 * B_HEADS, TQ), F32)],
        compiler_params=_params(2), name="attn_b")(bq, bk, bvt)


def _stick_front(z, u_ref, mask):
    log1p2 = jnp.log(1.0 + jnp.exp2(jnp.minimum(z, 126.0))) * LOG2E
    sp = jnp.maximum(z, log1p2)
    logsig = z - sp
    if mask is not None:
        sp = jnp.where(mask, sp, 0.0)
    tl = jnp.dot(u_ref[...], sp.astype(BF16), preferred_element_type=F32)
    return logsig, tl[0:1, :] + sp[0:1, :], tl


def _stick_back(front, vt, mask, c, tail_ref, acc_ref, rows):
    logsig, tile_mass, tl = front
    row = slice(8 * c, 8 * c + 1)
    tail = tail_ref[row, :]
    w = jnp.exp2(logsig - tl)
    if mask is not None:
        w = jnp.where(mask, w, 0.0)
    pv = jnp.dot(vt, w.astype(BF16), preferred_element_type=F32)
    acc_ref[rows] = acc_ref[rows] + pv * jnp.exp2(-tail)
    tail_ref[row, :] = tail + tile_mass


def _attn_c_kernel(q_ref, k_ref, vt_ref, u_ref, o_ref, s_ref, acc_ref, tail_ref):
    i = pl.program_id(1)
    n = C_HEADS

    def scores(j, h):
        k = k_ref[pl.ds(pl.multiple_of(j * TK, TK), TK), LANES * (h // 2):LANES * (h // 2 + 1)]
        return lax.dot_general(k, q_ref[:, LANES * h:LANES * (h + 1)], NT_DIMS,
                               preferred_element_type=F32)

    def phase(j, slot, mask, j_next):
        fronts = {}
        for c in range(n + STICK_LAG):
            if c < n:
                s_ref[1 - slot, c] = scores(j_next, c)
                fronts[c] = _stick_front(s_ref[slot, c], u_ref, mask)
            d = c - STICK_LAG
            if d >= 0:
                rows = slice(C_HEAD_DIM * d, C_HEAD_DIM * (d + 1))
                _stick_back(fronts.pop(d), vt_ref[j, rows, :], mask, d, tail_ref, acc_ref, rows)

    acc_ref[...] = jnp.zeros_like(acc_ref)
    state_row = lax.broadcasted_iota(jnp.int32, tail_ref.shape, 0)
    tail_ref[...] = jnp.where((state_row & 7) == 0, 0.0, -NEG_BIG)
    for c in range(n):
        s_ref[0, c] = scores(2 * i + 1, c)
    mask0, mask1 = _diag_masks(lambda kp, qp: kp < qp)
    phase(2 * i + 1, 0, mask1, 2 * i)
    phase(2 * i, 1, mask0, jnp.maximum(2 * i - 1, 0))

    def min_tail():
        return jnp.min(tail_ref[...])

    def cond(carry):
        u, tmin = carry
        return jnp.logical_and(u < i, tmin < DEAD_TAIL)

    def body(carry):
        u, _ = carry
        j = 2 * i - 1 - 2 * u
        phase(j, 0, None, j - 1)

        @pl.when(min_tail() < DEAD_TAIL)
        def _():
            phase(j - 1, 1, None, jnp.maximum(j - 2, 0))

        return u + 1, min_tail()

    lax.while_loop(cond, body, (jnp.int32(0), min_tail()))
    for hp in range(C_HEADS // 2):
        blk = slice(LANES * hp, LANES * (hp + 1))
        o_ref[:, blk] = acc_ref[blk].T.astype(BF16)


def _attn_c(q, k, vt, u):
    n_q = SEQ // TQ
    return pl.pallas_call(
        _attn_c_kernel, grid=(BATCH, n_q),
        in_specs=[pl.BlockSpec((None, TQ, 2 * C_WIDTH), lambda b, i: (b, i, 0)),
                  _kv_spec(C_WIDTH), _vt_spec(C_WIDTH), _resident(u.shape)],
        out_specs=pl.BlockSpec((None, TQ, C_WIDTH), lambda b, i: (b, i, 0)),
        out_shape=jax.ShapeDtypeStruct((BATCH, SEQ, C_WIDTH), BF16),
        scratch_shapes=[pltpu.VMEM((2, C_HEADS, TK, TQ), F32), pltpu.VMEM((C_WIDTH, TQ), F32),
                        pltpu.VMEM((8 * C_HEADS, TQ), F32)],
        compiler_params=_params(2), name="attn_c")(q, k, vt, u)


def _ffn_kernel(*refs):
    x_ref, part_refs = refs[0], refs[1:-7]
    wo_ref, g_ref, wg_ref, wu_ref, wd_ref, o_ref, a_ref = refs[-7:]
    x = x_ref[...]
    off = 0
    for p_ref in part_refs:
        n = p_ref.shape[1]
        x = x + jnp.dot(p_ref[...], wo_ref[off:off + n, :], preferred_element_type=F32)
        off += n
    hb = _rms(x, g_ref[...]).astype(BF16)
    for c in range(D_FF // FF_CHUNK):
        sl = slice(c * FF_CHUNK, (c + 1) * FF_CHUNK)
        g = jnp.dot(hb, wg_ref[:, sl], preferred_element_type=F32)
        u = jnp.dot(hb, wu_ref[:, sl], preferred_element_type=F32)
        a_ref[:, sl] = (g * (1.0 / (1.0 + jnp.exp(-g))) * u).astype(BF16)
    o_ref[...] = x + jnp.dot(a_ref[...], wd_ref[...], preferred_element_type=F32)


def _outproj_ffn(x2, parts, w_out, gain, wg, wu, wd):
    t = x2.shape[0]
    row = lambda n: pl.BlockSpec((TM_FFN, n), lambda i: (i, 0))
    weights = [w_out, gain, wg, wu, wd]
    return pl.pallas_call(
        _ffn_kernel, grid=(t // TM_FFN,),
        in_specs=([row(D_MODEL)] + [row(p.shape[1]) for p in parts]
                  + [_resident(w.shape) for w in weights]),
        out_specs=row(D_MODEL), out_shape=jax.ShapeDtypeStruct(x2.shape, F32),
        scratch_shapes=[pltpu.VMEM((TM_FFN, D_FF), BF16)],
        compiler_params=_params(1), name="outproj_ffn")(x2, *parts, *weights)


def _consts():
    lane = np.arange(LANES)
    pos = jnp.arange(SEQ, dtype=F32)[:, None]

    def table(half, idx, active, neg):
        inv_freq = jnp.power(ROPE_THETA, -jnp.arange(half, dtype=F32) / half)
        ang = pos * inv_freq[idx][None, :]
        act = jnp.asarray(active)[None, :]
        sign = jnp.asarray(np.where(neg, -1.0, 1.0), F32)[None, :]
        return jnp.where(act, jnp.cos(ang), 1.0), jnp.where(act, jnp.sin(ang) * sign, 0.0)

    cosa, sina = table(32, lane % 32, np.ones(LANES, bool), (lane % 64) < 32)
    rope_b = (lane >= 64) & (lane < 96)
    cosb, sinb = table(16, (lane - 64) % 16, rope_b, lane < 80)

    i256 = np.arange(MXU_DIM)
    ga = ((i256[:, None] // 64) == (i256[None, :] // 64)) / 64.0
    blk = i256 // LANES
    w = i256 % LANES
    grp = np.where(w < 64, 0, np.where(w < 96, 1, 2))
    same = (blk[:, None] == blk[None, :]) & (grp[:, None] == grp[None, :])
    scale = np.where(grp == 0, 1.0 / 64, np.where(grp == 1, 1.0 / 32, 0.0))
    gb = same * scale[None, :]
    it = np.arange(TK)
    u = (it[None, :] > it[:, None]).astype(np.float32)
    return dict(cosa=cosa, sina=sina, cosb=cosb, sinb=sinb,
                ga=jnp.asarray(ga, BF16), gb=jnp.asarray(gb, BF16), u=jnp.asarray(u, BF16))


def _pad_heads(wm, n_heads, width):
    k = wm.shape[0]
    w3 = wm.reshape(k, n_heads, width)
    return jnp.pad(w3, ((0, 0), (0, 0), (0, LANES - width))).reshape(k, n_heads * LANES)


def _tile_gain(parts, reps, scale=1.0):
    g = jnp.concatenate([p.astype(F32) for p in parts])
    g = jnp.pad(g, (0, LANES - g.shape[0]))
    return (jnp.tile(g, reps) * scale)[None, :]


def _even_weights(i, l, norm_mix, ab_w_in, a_q_norm, a_k_norm, a_out_norm, b_q_a_norm, b_w_q_b,
                  b_kv_a_norm, b_w_kv_b, b_q_nope_norm, b_q_rope_norm, b_k_nope_norm,
                  b_k_rope_norm):
    w_in = ab_w_in[i]
    kr_cols = jnp.pad(w_in[:, 1920:1952], ((0, 0), (64, 32)))
    wtok = jnp.concatenate([w_in[:, 0:1024], w_in[:, 1536:1920], kr_cols], axis=1).astype(BF16)
    wkv = b_w_kv_b[i].reshape(B_KV_RANK, B_HEADS, B_NOPE_DIM + B_V_DIM)
    sa = (A_QK_DIM ** -0.5) * LOG2E
    sb = ((B_NOPE_DIM + B_ROPE_DIM) ** -0.5) * LOG2E
    zeros32 = jnp.zeros((32,), F32)
    return dict(
        gmix=norm_mix[l][None, :],
        wtok=wtok,
        wavt=w_in[:, 1024:1536].T.astype(BF16),
        wqb=_pad_heads(b_w_q_b[i], B_HEADS, B_NOPE_DIM + B_ROPE_DIM).astype(BF16),
        wkb=_pad_heads(wkv[:, :, :B_NOPE_DIM].reshape(B_KV_RANK, -1), B_HEADS,
                       B_NOPE_DIM).astype(BF16),
        wbvt=wkv[:, :, B_NOPE_DIM:].reshape(B_KV_RANK, -1).T.astype(BF16),
        gaq=_tile_gain([a_q_norm[i], a_q_norm[i]], 4, sa),
        gak=_tile_gain([a_k_norm[i], a_k_norm[i]], 4),
        gqa=b_q_a_norm[i][None, :],
        gkva=b_kv_a_norm[i][None, :],
        gbq=_tile_gain([b_q_nope_norm[i], b_q_rope_norm[i]], 8, sb),
        gbk=_tile_gain([b_k_nope_norm[i]], 8),
        gkr=_tile_gain([jnp.zeros((64,), F32), b_k_rope_norm[i], zeros32], 1),
        gout=a_out_norm[i][None, :] * (1.0 - _lambda_init(l)),
    )


def kernel(x, norm_mix, norm_ffn, ab_w_in, a_q_norm, a_k_norm, a_lambda, a_out_norm, b_q_a_norm,
           b_w_q_b, b_kv_a_norm, b_w_kv_b, b_q_nope_norm, b_q_rope_norm, b_k_nope_norm,
           b_k_rope_norm, ab_w_out, c_w_in, c_w_out, ffn_w_gate, ffn_w_up, ffn_w_down):
    consts = _consts()
    t = BATCH * SEQ
    for l in range(DEPTH):
        i = l // 2
        x3 = x.reshape(BATCH, SEQ, D_MODEL)
        if l % 2 == 0:
            w = _even_weights(i, l, norm_mix, ab_w_in, a_q_norm, a_k_norm, a_out_norm, b_q_a_norm,
                              b_w_q_b, b_kv_a_norm, b_w_kv_b, b_q_nope_norm, b_q_rope_norm,
                              b_k_nope_norm, b_k_rope_norm)
            aq, ak, avt, bq, bk, bvt = _proj_even(x3, consts, w)
            out_a = _attn_a(aq, ak, avt, a_lambda[i], w["gout"], _lambda_init(l))
            out_b = _attn_b(bq, bk, bvt)
            parts = [out_a.reshape(t, 512), out_b.reshape(t, 512)]
            w_out = ab_w_out[i].astype(BF16)
        else:
            w_in = c_w_in[i]
            q, k, vt = _proj_odd(x3, norm_mix[l][None, :], w_in[:, :2 * C_WIDTH].astype(BF16),
                                 w_in[:, 2 * C_WIDTH:].T.astype(BF16))
            out_c = _attn_c(q, k, vt, consts["u"])
            parts = [out_c.reshape(t, C_WIDTH)]
            w_out = c_w_out[i].astype(BF16)
        x2 = _outproj_ffn(x.reshape(t, D_MODEL), parts, w_out, norm_ffn[l][None, :],
                          ffn_w_gate[l].astype(BF16), ffn_w_up[l].astype(BF16),
                          ffn_w_down[l].astype(BF16))
        x = x2.reshape(BATCH, SEQ, D_MODEL)
    return x
```

```python
import functools
import math

import jax
import jax.numpy as jnp
import numpy as np
from jax import lax
from jax.experimental import pallas as pl
from jax.experimental.pallas import tpu as pltpu

D_MODEL = 1024
BATCH = 16
SEQ = 2048
DEPTH = 4
CHUNK = 64
ROPE_THETA = 10000.0
NORM_EPS = 1e-6

A_HEADS = 4
A_QK_DIM = 64
A_V_DIM = 128
B_HEADS = 8
B_Q_RANK = 256
B_KV_RANK = 128
B_NOPE_DIM = 64
B_ROPE_DIM = 32
B_V_DIM = 64
C_HEADS = 16
C_HEAD_DIM = 64
C_WIDTH = C_HEADS * C_HEAD_DIM
D_FF = 2816

LANES = 128
MXU_DIM = 256
TQ = 512
TK = 256
TM = 512
TM_FFN = 512
FF_CHUNK = 256
VMEM_LIMIT_BYTES = 56 * 1024 * 1024
LOG2E = 1.4426950408889634
NEG_BIG = -1e30
DEAD_TAIL = 160.0
STICK_LAG = 1

F32 = jnp.float32
BF16 = jnp.bfloat16
NT_DIMS = (((1,), (1,)), ((), ()))


def _lambda_init(layer_idx):
    return 0.8 - 0.6 * math.exp(-0.3 * layer_idx)


def _params(n_axes):
    return pltpu.CompilerParams(dimension_semantics=("arbitrary",) * n_axes,
                                vmem_limit_bytes=VMEM_LIMIT_BYTES)


def _resident(shape):
    zeros = (0,) * len(shape)
    return pl.BlockSpec(shape, lambda *_: zeros, pipeline_mode=pl.Buffered(1))


def _rms(x, gain):
    ms = jnp.mean(x * x, axis=1, keepdims=True)
    return x * lax.rsqrt(ms + NORM_EPS) * gain


def _group_mean_sq(y, g_ref):
    outs = []
    for c in range(y.shape[1] // MXU_DIM):
        yc = y[:, c * MXU_DIM:(c + 1) * MXU_DIM]
        outs.append(jnp.dot((yc * yc).astype(BF16), g_ref[...], preferred_element_type=F32))
    return jnp.concatenate(outs, axis=1)


def _group_rms(y, g_ref, gain):
    return y * lax.rsqrt(_group_mean_sq(y, g_ref) + NORM_EPS) * gain


def _rope(y, first_half, half, cos, sin_signed):
    n = y.shape[1]
    partner = jnp.where(first_half, pltpu.roll(y, n - half, 1), pltpu.roll(y, half, 1))
    return y * cos + partner * sin_signed


def _split_halves(q):
    lo_lanes = lax.broadcasted_iota(jnp.int32, (q.shape[0], LANES), 1) < 64
    blocks = []
    for b in range(q.shape[1] // LANES):
        blk = q[:, b * LANES:(b + 1) * LANES]
        blocks += [jnp.where(lo_lanes, blk, 0.0), jnp.where(lo_lanes, 0.0, blk)]
    return jnp.concatenate(blocks, axis=1)


def _proj_even_kernel(x_ref, gmix_ref, wtok_ref, wavt_ref, wqb_ref, wkb_ref, wbvt_ref,
                      ga_ref, gb_ref, gaq_ref, gak_ref, gqa_ref, gkva_ref, gbq_ref, gbk_ref,
                      gkr_ref, cosa_ref, sina_ref, cosb_ref, sinb_ref,
                      aq_ref, ak_ref, avt_ref, bq_ref, bk_ref, bvt_ref):
    tm = x_ref.shape[0]
    hb = _rms(x_ref[...], gmix_ref[...]).astype(BF16)
    p = jnp.dot(hb, wtok_ref[...], preferred_element_type=F32)

    lane_a = lax.broadcasted_iota(jnp.int32, (tm, 512), 1)
    first_a = (lane_a & 32) == 0
    cosa = jnp.concatenate([cosa_ref[...]] * 4, axis=1)
    sina = jnp.concatenate([sina_ref[...]] * 4, axis=1)
    aq = _rope(_group_rms(p[:, 0:512], ga_ref, gaq_ref[...]), first_a, 32, cosa, sina)
    aq_ref[...] = _split_halves(aq).astype(BF16)
    ak = _rope(_group_rms(p[:, 512:1024], ga_ref, gak_ref[...]), first_a, 32, cosa, sina)
    ak_ref[...] = ak.astype(BF16)
    avt = lax.dot_general(wavt_ref[...], hb, NT_DIMS, preferred_element_type=F32)
    for c in range(tm // TK):
        avt_ref[c] = avt[:, c * TK:(c + 1) * TK].astype(BF16)

    qln = _rms(p[:, 1024:1280], gqa_ref[...]).astype(BF16)
    qb = jnp.dot(qln, wqb_ref[...], preferred_element_type=F32)
    lane_b = lax.broadcasted_iota(jnp.int32, (tm, 1024), 1)
    first_b = (lane_b & 127) < 80
    cosb = jnp.concatenate([cosb_ref[...]] * 8, axis=1)
    sinb = jnp.concatenate([sinb_ref[...]] * 8, axis=1)
    bq = _rope(_group_rms(qb, gb_ref, gbq_ref[...]), first_b, 16, cosb, sinb)
    bq_ref[...] = bq.astype(BF16)

    kvn = _rms(p[:, 1280:1408], gkva_ref[...]).astype(BF16)
    kb = _group_rms(jnp.dot(kvn, wkb_ref[...], preferred_element_type=F32),
                    gb_ref, gbk_ref[...])
    kr = p[:, 1408:1536]
    krn = kr * lax.rsqrt(jnp.sum(kr * kr, axis=1, keepdims=True) * (1.0 / B_ROPE_DIM)
                         + NORM_EPS) * gkr_ref[...]
    lane_r = lax.broadcasted_iota(jnp.int32, (tm, 128), 1)
    krr = _rope(krn, lane_r < 80, 16, cosb_ref[...], sinb_ref[...])
    bk_ref[...] = (kb + jnp.concatenate([krr] * 8, axis=1)).astype(BF16)
    bvt = lax.dot_general(wbvt_ref[...], kvn, NT_DIMS, preferred_element_type=F32)
    for c in range(tm // TK):
        bvt_ref[c] = bvt[:, c * TK:(c + 1) * TK].astype(BF16)


def _proj_even(x3, consts, w):
    n_s = SEQ // TM
    grid = (BATCH * n_s,)
    row = lambda n: pl.BlockSpec((None, TM, n), lambda i: (i // n_s, i % n_s, 0))
    vt = lambda n: pl.BlockSpec((None, TM // TK, n, TK), lambda i: (i // n_s, i % n_s, 0, 0))
    tab = pl.BlockSpec((TM, LANES), lambda i: (i % n_s, 0))
    ins = [x3, w["gmix"], w["wtok"], w["wavt"], w["wqb"], w["wkb"], w["wbvt"],
           consts["ga"], consts["gb"], w["gaq"], w["gak"], w["gqa"], w["gkva"], w["gbq"],
           w["gbk"], w["gkr"], consts["cosa"], consts["sina"], consts["cosb"], consts["sinb"]]
    in_specs = [row(D_MODEL)] + [_resident(a.shape) for a in ins[1:16]] + [tab] * 4
    out_shape = [
        jax.ShapeDtypeStruct((BATCH, SEQ, 1024), BF16),
        jax.ShapeDtypeStruct((BATCH, SEQ, 512), BF16),
        jax.ShapeDtypeStruct((BATCH, SEQ // TK, 512, TK), BF16),
        jax.ShapeDtypeStruct((BATCH, SEQ, 1024), BF16),
        jax.ShapeDtypeStruct((BATCH, SEQ, 1024), BF16),
        jax.ShapeDtypeStruct((BATCH, SEQ // TK, 512, TK), BF16),
    ]
    out_specs = [row(1024), row(512), vt(512), row(1024), row(1024), vt(512)]
    return pl.pallas_call(_proj_even_kernel, grid=grid, in_specs=in_specs, out_specs=out_specs,
                          out_shape=out_shape, compiler_params=_params(1),
                          name="proj_even")(*ins)


def _proj_odd_kernel(x_ref, gmix_ref, wqk_ref, wvt_ref, q_ref, k_ref, vt_ref):
    tm = x_ref.shape[0]
    hb = _rms(x_ref[...], gmix_ref[...]).astype(BF16)
    qk = jnp.dot(hb, wqk_ref[...], preferred_element_type=F32)
    q_ref[...] = _split_halves(qk[:, :C_WIDTH] * (C_HEAD_DIM ** -0.5 * LOG2E)).astype(BF16)
    k_ref[...] = qk[:, C_WIDTH:].astype(BF16)
    vt = lax.dot_general(wvt_ref[...], hb, NT_DIMS, preferred_element_type=F32)
    for c in range(tm // TK):
        vt_ref[c] = vt[:, c * TK:(c + 1) * TK].astype(BF16)


def _proj_odd(x3, gmix, wqk, wvt):
    n_s = SEQ // TM
    row = lambda n: pl.BlockSpec((None, TM, n), lambda i: (i // n_s, i % n_s, 0))
    vt = pl.BlockSpec((None, TM // TK, C_WIDTH, TK), lambda i: (i // n_s, i % n_s, 0, 0))
    out_shape = [
        jax.ShapeDtypeStruct((BATCH, SEQ, 2 * C_WIDTH), BF16),
        jax.ShapeDtypeStruct((BATCH, SEQ, C_WIDTH), BF16),
        jax.ShapeDtypeStruct((BATCH, SEQ // TK, C_WIDTH, TK), BF16),
    ]
    return pl.pallas_call(
        _proj_odd_kernel, grid=(BATCH * n_s,),
        in_specs=[row(D_MODEL), _resident(gmix.shape), _resident(wqk.shape), _resident(wvt.shape)],
        out_specs=[row(2 * C_WIDTH), row(C_WIDTH), vt], out_shape=out_shape,
        compiler_params=_params(1), name="proj_odd")(x3, gmix, wqk, wvt)


def _diag_masks(fn):
    r = lax.broadcasted_iota(jnp.int32, (TK, TQ), 0)
    c = lax.broadcasted_iota(jnp.int32, (TK, TQ), 1)
    return fn(r, c), fn(r + TK, c)


def _softmax_rest(s, vt, mask, c, ql, m_ref, l_ref, acc_ref, rows):
    if mask is not None:
        s = jnp.where(mask, s, -jnp.inf)
    row = slice(8 * c, 8 * c + 1)
    m = m_ref[row, ql]
    m_new = jnp.maximum(m, jnp.max(s, axis=0, keepdims=True))
    alpha = jnp.exp2(m - m_new)
    p = jnp.exp2(s - m_new)
    l_ref[row, ql] = alpha * l_ref[row, ql] + jnp.sum(p, axis=0, keepdims=True)
    m_ref[row, ql] = m_new
    acc_ref[rows, ql] = (alpha * acc_ref[rows, ql]
                         + jnp.dot(vt, p.astype(BF16), preferred_element_type=F32))


ALL_Q = slice(0, TQ)
EARLY_Q = slice(0, TQ // 2)
LATE_Q = slice(TQ // 2, TQ)


def _softmax_attention(i, n, scores, rest, s_ref, m_ref, l_ref, acc_ref):
    acc_ref[...] = jnp.zeros_like(acc_ref)
    m_ref[...] = jnp.full_like(m_ref, NEG_BIG)
    l_ref[...] = jnp.zeros_like(l_ref)

    def phase(j, slot, ql, mask, next_ql):
        for c in range(n):
            if next_ql is not None:
                s_ref[1 - slot, c, :, next_ql] = scores(j + 1, c, next_ql)
            rest(j, c, ql, s_ref[slot, c, :, ql], None if mask is None else mask[:, ql])

    for c in range(n):
        s_ref[0, c] = scores(0, c, ALL_Q)

    def body(u, carry):
        phase(2 * u, 0, ALL_Q, None, ALL_Q)
        phase(2 * u + 1, 1, ALL_Q, None, ALL_Q)
        return carry

    lax.fori_loop(0, i, body, 0)
    mask0, mask1 = _diag_masks(lambda kp, qp: (kp >> 6) <= (qp >> 6))
    phase(2 * i, 0, ALL_Q, mask0, LATE_Q)
    phase(2 * i + 1, 1, LATE_Q, mask1, None)


def _attn_a_kernel(q_ref, k_ref, vt_ref, lam_ref, gout_ref, o_ref, s_ref, acc_ref, m_ref, l_ref,
                   *, lam0):
    i = pl.program_id(1)

    def scores(j, c, ql):
        h = c // 2
        k = k_ref[pl.ds(pl.multiple_of(j * TK, TK), TK), LANES * h:LANES * (h + 1)]
        return lax.dot_general(k, q_ref[ql, LANES * c:LANES * (c + 1)], NT_DIMS,
                               preferred_element_type=F32)

    def rest(j, c, ql, s, mask):
        h = c // 2
        _softmax_rest(s, vt_ref[j, LANES * h:LANES * (h + 1), :], mask, c, ql, m_ref, l_ref,
                      acc_ref, slice(LANES * c, LANES * (c + 1)))

    _softmax_attention(i, 2 * A_HEADS, scores, rest, s_ref, m_ref, l_ref, acc_ref)
    al = lam_ref[...]
    lam = (jnp.exp(jnp.sum(al[0:1] * al[1:2], axis=1, keepdims=True))
           - jnp.exp(jnp.sum(al[2:3] * al[3:4], axis=1, keepdims=True)) + lam0)
    for h in range(A_HEADS):
        c1, c2 = 2 * h, 2 * h + 1
        o1 = acc_ref[LANES * c1:LANES * (c1 + 1)] * (1.0 / l_ref[8 * c1:8 * c1 + 1, :])
        o2 = acc_ref[LANES * c2:LANES * (c2 + 1)] * (1.0 / l_ref[8 * c2:8 * c2 + 1, :])
        o = o1 - lam * o2
        o_ref[:, LANES * h:LANES * (h + 1)] = _rms(o.T, gout_ref[...]).astype(BF16)


def _kv_spec(n_lanes):
    return pl.BlockSpec((None, SEQ, n_lanes), lambda b, i: (b, 0, 0), pipeline_mode=pl.Buffered(1))


def _vt_spec(n_rows):
    return pl.BlockSpec((None, SEQ // TK, n_rows, TK), lambda b, i: (b, 0, 0, 0),
                        pipeline_mode=pl.Buffered(1))


def _attn_a(aq, ak, avt, a_lambda, gout, lam0):
    n_q = SEQ // TQ
    n_chain = 2 * A_HEADS
    kern = functools.partial(_attn_a_kernel, lam0=lam0)
    return pl.pallas_call(
        kern, grid=(BATCH, n_q),
        in_specs=[pl.BlockSpec((None, TQ, n_chain * LANES), lambda b, i: (b, i, 0)),
                  _kv_spec(512), _vt_spec(512),
                  _resident(a_lambda.shape), _resident(gout.shape)],
        out_specs=pl.BlockSpec((None, TQ, 512), lambda b, i: (b, i, 0)),
        out_shape=jax.ShapeDtypeStruct((BATCH, SEQ, 512), BF16),
        scratch_shapes=[pltpu.VMEM((2, n_chain, TK, TQ), F32),
                        pltpu.VMEM((n_chain * A_V_DIM, TQ), F32),
                        pltpu.VMEM((8 * n_chain, TQ), F32), pltpu.VMEM((8 * n_chain, TQ), F32)],
        compiler_params=_params(2), name="attn_a")(aq, ak, avt, a_lambda, gout)


def _attn_b_kernel(q_ref, k_ref, vt_ref, o_ref, s_ref, acc_ref, m_ref, l_ref):
    i = pl.program_id(1)

    def scores(j, h, ql):
        cols = slice(LANES * h, LANES * (h + 1))
        return lax.dot_general(k_ref[pl.ds(pl.multiple_of(j * TK, TK), TK), cols], q_ref[ql, cols],
                               NT_DIMS, preferred_element_type=F32)

    def rest(j, h, ql, s, mask):
        rows = slice(B_V_DIM * h, B_V_DIM * (h + 1))
        _softmax_rest(s, vt_ref[j, rows, :], mask, h, ql, m_ref, l_ref, acc_ref, rows)

    _softmax_attention(i, B_HEADS, scores, rest, s_ref, m_ref, l_ref, acc_ref)
    for hp in range(B_HEADS // 2):
        parts = []
        for h in (2 * hp, 2 * hp + 1):
            rows = slice(B_V_DIM * h, B_V_DIM * (h + 1))
            parts.append(acc_ref[rows] * (1.0 / l_ref[8 * h:8 * h + 1, :]))
        o_ref[:, LANES * hp:LANES * (hp + 1)] = jnp.concatenate(parts, axis=0).T.astype(BF16)


def _attn_b(bq, bk, bvt):
    n_q = SEQ // TQ
    return pl.pallas_call(
        _attn_b_kernel, grid=(BATCH, n_q),
        in_specs=[pl.BlockSpec((None, TQ, 1024), lambda b, i: (b, i, 0)),
                  _kv_spec(1024), _vt_spec(512)],
        out_specs=pl.BlockSpec((None, TQ, 512), lambda b, i: (b, i, 0)),
        out_shape=jax.ShapeDtypeStruct((BATCH, SEQ, 512), BF16),
        scratch_shapes=[pltpu.VMEM((2, B_HEADS, TK, TQ), F32),
                        pltpu.VMEM((B_HEADS * B_V_DIM, TQ), F32),
                        pltpu.VMEM((8 * B_HEADS, TQ), F32), pltpu.VMEM((8 * B_HEADS, TQ), F32)],
        compiler_params=_params(2), name="attn_b")(bq, bk, bvt)


def _stick_front(z, u_ref, mask):
    log1p2 = jnp.log(1.0 + jnp.exp2(jnp.minimum(z, 126.0))) * LOG2E
    sp = jnp.maximum(z, log1p2)
    logsig = z - sp
    if mask is not None:
        sp = jnp.where(mask, sp, 0.0)
    tl = jnp.dot(u_ref[...], sp.astype(BF16), preferred_element_type=F32)
    return logsig, tl[0:1, :] + sp[0:1, :], tl


def _stick_back(front, vt, mask, c, tail_ref, acc_ref, rows, ql):
    logsig, tile_mass, tl = front
    row = slice(8 * c, 8 * c + 1)
    tail = tail_ref[row, ql]
    w = jnp.exp2(logsig - tl)
    if mask is not None:
        w = jnp.where(mask, w, 0.0)
    pv = jnp.dot(vt, w.astype(BF16), preferred_element_type=F32)
    acc_ref[rows, ql] = acc_ref[rows, ql] + pv * jnp.exp2(-tail)
    tail_ref[row, ql] = tail + tile_mass


def _attn_c_kernel(q_ref, k_ref, vt_ref, u_ref, o_ref, s_ref, acc_ref, tail_ref):
    i = pl.program_id(1)
    n = C_HEADS

    def scores(j, h, ql):
        k = k_ref[pl.ds(pl.multiple_of(j * TK, TK), TK), LANES * (h // 2):LANES * (h // 2 + 1)]
        return lax.dot_general(k, q_ref[ql, LANES * h:LANES * (h + 1)], NT_DIMS,
                               preferred_element_type=F32)

    def phase(j, slot, ql, mask, j_next, next_ql=ALL_Q):
        fronts = {}
        qmask = None if mask is None else mask[:, ql]
        for c in range(n + STICK_LAG):
            if c < n:
                s_ref[1 - slot, c, :, next_ql] = scores(j_next, c, next_ql)
                fronts[c] = _stick_front(s_ref[slot, c, :, ql], u_ref, qmask)
            d = c - STICK_LAG
            if d >= 0:
                rows = slice(C_HEAD_DIM * d, C_HEAD_DIM * (d + 1))
                _stick_back(fronts.pop(d), vt_ref[j, rows, :], qmask, d, tail_ref, acc_ref,
                            rows, ql)

    acc_ref[...] = jnp.zeros_like(acc_ref)
    state_row = lax.broadcasted_iota(jnp.int32, tail_ref.shape, 0)
    tail_ref[...] = jnp.where((state_row & 7) == 0, 0.0, -NEG_BIG)
    for c in range(n):
        s_ref[0, c, :, LATE_Q] = scores(2 * i + 1, c, LATE_Q)
    mask0, mask1 = _diag_masks(lambda kp, qp: kp < qp)
    phase(2 * i + 1, 0, LATE_Q, mask1, 2 * i)
    phase(2 * i, 1, ALL_Q, mask0, jnp.maximum(2 * i - 1, 0))

    def min_tail(ql=ALL_Q):
        return jnp.min(tail_ref[:, ql])

    def live_phase(j, slot, j_next):
        late_alive = min_tail(LATE_Q) < DEAD_TAIL

        @pl.when(late_alive)
        def _():
            phase(j, slot, ALL_Q, None, j_next, ALL_Q)

        @pl.when(jnp.logical_not(late_alive))
        def _():
            phase(j, slot, EARLY_Q, None, j_next, EARLY_Q)

    def cond(carry):
        u, tmin = carry
        return jnp.logical_and(u < i, tmin < DEAD_TAIL)

    def body(carry):
        u, _ = carry
        j = 2 * i - 1 - 2 * u
        live_phase(j, 0, j - 1)

        @pl.when(min_tail() < DEAD_TAIL)
        def _():
            live_phase(j - 1, 1, jnp.maximum(j - 2, 0))

        return u + 1, min_tail()

    lax.while_loop(cond, body, (jnp.int32(0), min_tail()))
    for hp in range(C_HEADS // 2):
        blk = slice(LANES * hp, LANES * (hp + 1))
        o_ref[:, blk] = acc_ref[blk].T.astype(BF16)


def _attn_c(q, k, vt, u):
    n_q = SEQ // TQ
    return pl.pallas_call(
        _attn_c_kernel, grid=(BATCH, n_q),
        in_specs=[pl.BlockSpec((None, TQ, 2 * C_WIDTH), lambda b, i: (b, i, 0)),
                  _kv_spec(C_WIDTH), _vt_spec(C_WIDTH), _resident(u.shape)],
        out_specs=pl.BlockSpec((None, TQ, C_WIDTH), lambda b, i: (b, i, 0)),
        out_shape=jax.ShapeDtypeStruct((BATCH, SEQ, C_WIDTH), BF16),
        scratch_shapes=[pltpu.VMEM((2, C_HEADS, TK, TQ), F32), pltpu.VMEM((C_WIDTH, TQ), F32),
                        pltpu.VMEM((8 * C_HEADS, TQ), F32)],
        compiler_params=_params(2), name="attn_c")(q, k, vt, u)


def _ffn_kernel(*refs):
    x_ref, part_refs = refs[0], refs[1:-7]
    wo_ref, g_ref, wg_ref, wu_ref, wd_ref, o_ref, a_ref = refs[-7:]
    x = x_ref[...]
    off = 0
    for p_ref in part_refs:
        n = p_ref.shape[1]
        x = x + jnp.dot(p_ref[...], wo_ref[off:off + n, :], preferred_element_type=F32)
        off += n
    hb = _rms(x, g_ref[...]).astype(BF16)
    for c in range(D_FF // FF_CHUNK):
        sl = slice(c * FF_CHUNK, (c + 1) * FF_CHUNK)
        g = jnp.dot(hb, wg_ref[:, sl], preferred_element_type=F32)
        u = jnp.dot(hb, wu_ref[:, sl], preferred_element_type=F32)
        a_ref[:, sl] = (g * (1.0 / (1.0 + jnp.exp(-g))) * u).astype(BF16)
    o_ref[...] = x + jnp.dot(a_ref[...], wd_ref[...], preferred_element_type=F32)


def _outproj_ffn(x2, parts, w_out, gain, wg, wu, wd):
    t = x2.shape[0]
    row = lambda n: pl.BlockSpec((TM_FFN, n), lambda i: (i, 0))
    weights = [w_out, gain, wg, wu, wd]
    return pl.pallas_call(
        _ffn_kernel, grid=(t // TM_FFN,),
        in_specs=([row(D_MODEL)] + [row(p.shape[1]) for p in parts]
                  + [_resident(w.shape) for w in weights]),
        out_specs=row(D_MODEL), out_shape=jax.ShapeDtypeStruct(x2.shape, F32),
        scratch_shapes=[pltpu.VMEM((TM_FFN, D_FF), BF16)],
        compiler_params=_params(1), name="outproj_ffn")(x2, *parts, *weights)


def _consts():
    lane = np.arange(LANES)
    pos = jnp.arange(SEQ, dtype=F32)[:, None]

    def table(half, idx, active, neg):
        inv_freq = jnp.power(ROPE_THETA, -jnp.arange(half, dtype=F32) / half)
        ang = pos * inv_freq[idx][None, :]
        act = jnp.asarray(active)[None, :]
        sign = jnp.asarray(np.where(neg, -1.0, 1.0), F32)[None, :]
        return jnp.where(act, jnp.cos(ang), 1.0), jnp.where(act, jnp.sin(ang) * sign, 0.0)

    cosa, sina = table(32, lane % 32, np.ones(LANES, bool), (lane % 64) < 32)
    rope_b = (lane >= 64) & (lane < 96)
    cosb, sinb = table(16, (lane - 64) % 16, rope_b, lane < 80)

    i256 = np.arange(MXU_DIM)
    ga = ((i256[:, None] // 64) == (i256[None, :] // 64)) / 64.0
    blk = i256 // LANES
    w = i256 % LANES
    grp = np.where(w < 64, 0, np.where(w < 96, 1, 2))
    same = (blk[:, None] == blk[None, :]) & (grp[:, None] == grp[None, :])
    scale = np.where(grp == 0, 1.0 / 64, np.where(grp == 1, 1.0 / 32, 0.0))
    gb = same * scale[None, :]
    it = np.arange(TK)
    u = (it[None, :] > it[:, None]).astype(np.float32)
    return dict(cosa=cosa, sina=sina, cosb=cosb, sinb=sinb,
                ga=jnp.asarray(ga, BF16), gb=jnp.asarray(gb, BF16), u=jnp.asarray(u, BF16))


def _pad_heads(wm, n_heads, width):
    k = wm.shape[0]
    w3 = wm.reshape(k, n_heads, width)
    return jnp.pad(w3, ((0, 0), (0, 0), (0, LANES - width))).reshape(k, n_heads * LANES)


def _tile_gain(parts, reps, scale=1.0):
    g = jnp.concatenate([p.astype(F32) for p in parts])
    g = jnp.pad(g, (0, LANES - g.shape[0]))
    return (jnp.tile(g, reps) * scale)[None, :]


def _even_weights(i, l, norm_mix, ab_w_in, a_q_norm, a_k_norm, a_out_norm, b_q_a_norm, b_w_q_b,
                  b_kv_a_norm, b_w_kv_b, b_q_nope_norm, b_q_rope_norm, b_k_nope_norm,
                  b_k_rope_norm):
    w_in = ab_w_in[i]
    kr_cols = jnp.pad(w_in[:, 1920:1952], ((0, 0), (64, 32)))
    wtok = jnp.concatenate([w_in[:, 0:1024], w_in[:, 1536:1920], kr_cols], axis=1).astype(BF16)
    wkv = b_w_kv_b[i].reshape(B_KV_RANK, B_HEADS, B_NOPE_DIM + B_V_DIM)
    sa = (A_QK_DIM ** -0.5) * LOG2E
    sb = ((B_NOPE_DIM + B_ROPE_DIM) ** -0.5) * LOG2E
    zeros32 = jnp.zeros((32,), F32)
    return dict(
        gmix=norm_mix[l][None, :],
        wtok=wtok,
        wavt=w_in[:, 1024:1536].T.astype(BF16),
        wqb=_pad_heads(b_w_q_b[i], B_HEADS, B_NOPE_DIM + B_ROPE_DIM).astype(BF16),
        wkb=_pad_heads(wkv[:, :, :B_NOPE_DIM].reshape(B_KV_RANK, -1), B_HEADS,
                       B_NOPE_DIM).astype(BF16),
        wbvt=wkv[:, :, B_NOPE_DIM:].reshape(B_KV_RANK, -1).T.astype(BF16),
        gaq=_tile_gain([a_q_norm[i], a_q_norm[i]], 4, sa),
        gak=_tile_gain([a_k_norm[i], a_k_norm[i]], 4),
        gqa=b_q_a_norm[i][None, :],
        gkva=b_kv_a_norm[i][None, :],
        gbq=_tile_gain([b_q_nope_norm[i], b_q_rope_norm[i]], 8, sb),
        gbk=_tile_gain([b_k_nope_norm[i]], 8),
        gkr=_tile_gain([jnp.zeros((64,), F32), b_k_rope_norm[i], zeros32], 1),
        gout=a_out_norm[i][None, :] * (1.0 - _lambda_init(l)),
    )


def kernel(x, norm_mix, norm_ffn, ab_w_in, a_q_norm, a_k_norm, a_lambda, a_out_norm, b_q_a_norm,
           b_w_q_b, b_kv_a_norm, b_w_kv_b, b_q_nope_norm, b_q_rope_norm, b_k_nope_norm,
           b_k_rope_norm, ab_w_out, c_w_in, c_w_out, ffn_w_gate, ffn_w_up, ffn_w_down):
    consts = _consts()
    t = BATCH * SEQ
    for l in range(DEPTH):
        i = l // 2
        x3 = x.reshape(BATCH, SEQ, D_MODEL)
        if l % 2 == 0:
            w = _even_weights(i, l, norm_mix, ab_w_in, a_q_norm, a_k_norm, a_out_norm, b_q_a_norm,
                              b_w_q_b, b_kv_a_norm, b_w_kv_b, b_q_nope_norm, b_q_rope_norm,
                              b_k_nope_norm, b_k_rope_norm)
            aq, ak, avt, bq, bk, bvt = _proj_even(x3, consts, w)
            out_a = _attn_a(aq, ak, avt, a_lambda[i], w["gout"], _lambda_init(l))
            out_b = _attn_b(bq, bk, bvt)
            parts = [out_a.reshape(t, 512), out_b.reshape(t, 512)]
            w_out = ab_w_out[i].astype(BF16)
        else:
            w_in = c_w_in[i]
            q, k, vt = _proj_odd(x3, norm_mix[l][None, :], w_in[:, :2 * C_WIDTH].astype(BF16),
                                 w_in[:, 2 * C_WIDTH:].T.astype(BF16))
            out_c = _attn_c(q, k, vt, consts["u"])
            parts = [out_c.reshape(t, C_WIDTH)]
            w_out = c_w_out[i].astype(BF16)
        x2 = _outproj_ffn(x.reshape(t, D_MODEL), parts, w_out, norm_ffn[l][None, :],
                          ffn_w_gate[l].astype(BF16), ffn_w_up[l].astype(BF16),
                          ffn_w_down[l].astype(BF16))
        x = x2.reshape(BATCH, SEQ, D_MODEL)
    return x
```

```python
import functools
import math

import jax
import jax.numpy as jnp
import numpy as np
from jax import lax
from jax.experimental import pallas as pl
from jax.experimental.pallas import tpu as pltpu

D_MODEL = 1024
BATCH = 16
SEQ = 2048
DEPTH = 4
CHUNK = 64
ROPE_THETA = 10000.0
NORM_EPS = 1e-6

A_HEADS = 4
A_QK_DIM = 64
A_V_DIM = 128
B_HEADS = 8
B_Q_RANK = 256
B_KV_RANK = 128
B_NOPE_DIM = 64
B_ROPE_DIM = 32
B_V_DIM = 64
C_HEADS = 16
C_HEAD_DIM = 64
C_WIDTH = C_HEADS * C_HEAD_DIM
D_FF = 2816

LANES = 128
MXU_DIM = 256
TQ = 512
TK = 256
TM = 512
TM_FFN = 512
FF_CHUNK = 256
VMEM_LIMIT_BYTES = 56 * 1024 * 1024
LOG2E = 1.4426950408889634
NEG_BIG = -1e30
DEAD_TAIL = 160.0
STICK_LAG = 1

F32 = jnp.float32
BF16 = jnp.bfloat16
NT_DIMS = (((1,), (1,)), ((), ()))


def _lambda_init(layer_idx):
    return 0.8 - 0.6 * math.exp(-0.3 * layer_idx)


def _params(n_axes):
    return pltpu.CompilerParams(dimension_semantics=("arbitrary",) * n_axes,
                                vmem_limit_bytes=VMEM_LIMIT_BYTES)


def _resident(shape):
    zeros = (0,) * len(shape)
    return pl.BlockSpec(shape, lambda *_: zeros, pipeline_mode=pl.Buffered(1))


def _rms(x, gain):
    ms = jnp.mean(x * x, axis=1, keepdims=True)
    return x * lax.rsqrt(ms + NORM_EPS) * gain


def _group_mean_sq(y, g_ref):
    outs = []
    for c in range(y.shape[1] // MXU_DIM):
        yc = y[:, c * MXU_DIM:(c + 1) * MXU_DIM]
        outs.append(jnp.dot((yc * yc).astype(BF16), g_ref[...], preferred_element_type=F32))
    return jnp.concatenate(outs, axis=1)


def _group_rms(y, g_ref, gain):
    return y * lax.rsqrt(_group_mean_sq(y, g_ref) + NORM_EPS) * gain


def _rope(y, first_half, half, cos, sin_signed):
    n = y.shape[1]
    partner = jnp.where(first_half, pltpu.roll(y, n - half, 1), pltpu.roll(y, half, 1))
    return y * cos + partner * sin_signed


def _split_halves(q):
    lo_lanes = lax.broadcasted_iota(jnp.int32, (q.shape[0], LANES), 1) < 64
    blocks = []
    for b in range(q.shape[1] // LANES):
        blk = q[:, b * LANES:(b + 1) * LANES]
        blocks += [jnp.where(lo_lanes, blk, 0.0), jnp.where(lo_lanes, 0.0, blk)]
    return jnp.concatenate(blocks, axis=1)


def _proj_even_kernel(x_ref, gmix_ref, wtok_ref, wavt_ref, wqb_ref, wkb_ref, wbvt_ref,
                      ga_ref, gb_ref, gaq_ref, gak_ref, gqa_ref, gkva_ref, gbq_ref, gbk_ref,
                      gkr_ref, cosa_ref, sina_ref, cosb_ref, sinb_ref,
                      aq_ref, ak_ref, avt_ref, bq_ref, bk_ref, bvt_ref):
    tm = x_ref.shape[0]
    hb = _rms(x_ref[...], gmix_ref[...]).astype(BF16)
    p = jnp.dot(hb, wtok_ref[...], preferred_element_type=F32)

    lane_a = lax.broadcasted_iota(jnp.int32, (tm, 512), 1)
    first_a = (lane_a & 32) == 0
    cosa = jnp.concatenate([cosa_ref[...]] * 4, axis=1)
    sina = jnp.concatenate([sina_ref[...]] * 4, axis=1)
    aq = _rope(_group_rms(p[:, 0:512], ga_ref, gaq_ref[...]), first_a, 32, cosa, sina)
    aq_ref[...] = _split_halves(aq).astype(BF16)
    ak = _rope(_group_rms(p[:, 512:1024], ga_ref, gak_ref[...]), first_a, 32, cosa, sina)
    ak_ref[...] = ak.astype(BF16)
    avt = lax.dot_general(wavt_ref[...], hb, NT_DIMS, preferred_element_type=F32)
    for c in range(tm // TK):
        avt_ref[c] = avt[:, c * TK:(c + 1) * TK].astype(BF16)

    qln = _rms(p[:, 1024:1280], gqa_ref[...]).astype(BF16)
    qb = jnp.dot(qln, wqb_ref[...], preferred_element_type=F32)
    lane_b = lax.broadcasted_iota(jnp.int32, (tm, 1024), 1)
    first_b = (lane_b & 127) < 80
    cosb = jnp.concatenate([cosb_ref[...]] * 8, axis=1)
    sinb = jnp.concatenate([sinb_ref[...]] * 8, axis=1)
    bq = _rope(_group_rms(qb, gb_ref, gbq_ref[...]), first_b, 16, cosb, sinb)
    bq_ref[...] = bq.astype(BF16)

    kvn = _rms(p[:, 1280:1408], gkva_ref[...]).astype(BF16)
    kb = _group_rms(jnp.dot(kvn, wkb_ref[...], preferred_element_type=F32),
                    gb_ref, gbk_ref[...])
    kr = p[:, 1408:1536]
    krn = kr * lax.rsqrt(jnp.sum(kr * kr, axis=1, keepdims=True) * (1.0 / B_ROPE_DIM)
                         + NORM_EPS) * gkr_ref[...]
    lane_r = lax.broadcasted_iota(jnp.int32, (tm, 128), 1)
    krr = _rope(krn, lane_r < 80, 16, cosb_ref[...], sinb_ref[...])
    bk_ref[...] = (kb + jnp.concatenate([krr] * 8, axis=1)).astype(BF16)
    bvt = lax.dot_general(wbvt_ref[...], kvn, NT_DIMS, preferred_element_type=F32)
    for c in range(tm // TK):
        bvt_ref[c] = bvt[:, c * TK:(c + 1) * TK].astype(BF16)


def _proj_even(x3, consts, w):
    n_s = SEQ // TM
    grid = (BATCH * n_s,)
    row = lambda n: pl.BlockSpec((None, TM, n), lambda i: (i // n_s, i % n_s, 0))
    vt = lambda n: pl.BlockSpec((None, TM // TK, n, TK), lambda i: (i // n_s, i % n_s, 0, 0))
    tab = pl.BlockSpec((TM, LANES), lambda i: (i % n_s, 0))
    ins = [x3, w["gmix"], w["wtok"], w["wavt"], w["wqb"], w["wkb"], w["wbvt"],
           consts["ga"], consts["gb"], w["gaq"], w["gak"], w["gqa"], w["gkva"], w["gbq"],
           w["gbk"], w["gkr"], consts["cosa"], consts["sina"], consts["cosb"], consts["sinb"]]
    in_specs = [row(D_MODEL)] + [_resident(a.shape) for a in ins[1:16]] + [tab] * 4
    out_shape = [
        jax.ShapeDtypeStruct((BATCH, SEQ, 1024), BF16),
        jax.ShapeDtypeStruct((BATCH, SEQ, 512), BF16),
        jax.ShapeDtypeStruct((BATCH, SEQ // TK, 512, TK), BF16),
        jax.ShapeDtypeStruct((BATCH, SEQ, 1024), BF16),
        jax.ShapeDtypeStruct((BATCH, SEQ, 1024), BF16),
        jax.ShapeDtypeStruct((BATCH, SEQ // TK, 512, TK), BF16),
    ]
    out_specs = [row(1024), row(512), vt(512), row(1024), row(1024), vt(512)]
    return pl.pallas_call(_proj_even_kernel, grid=grid, in_specs=in_specs, out_specs=out_specs,
                          out_shape=out_shape, compiler_params=_params(1),
                          name="proj_even")(*ins)


def _proj_odd_kernel(x_ref, gmix_ref, wqk_ref, wvt_ref, q_ref, k_ref, vt_ref):
    tm = x_ref.shape[0]
    hb = _rms(x_ref[...], gmix_ref[...]).astype(BF16)
    qk = jnp.dot(hb, wqk_ref[...], preferred_element_type=F32)
    q_ref[...] = _split_halves(qk[:, :C_WIDTH] * (C_HEAD_DIM ** -0.5 * LOG2E)).astype(BF16)
    k_ref[...] = qk[:, C_WIDTH:].astype(BF16)
    vt = lax.dot_general(wvt_ref[...], hb, NT_DIMS, preferred_element_type=F32)
    for c in range(tm // TK):
        vt_ref[c] = vt[:, c * TK:(c + 1) * TK].astype(BF16)


def _proj_odd(x3, gmix, wqk, wvt):
    n_s = SEQ // TM
    row = lambda n: pl.BlockSpec((None, TM, n), lambda i: (i // n_s, i % n_s, 0))
    vt = pl.BlockSpec((None, TM // TK, C_WIDTH, TK), lambda i: (i // n_s, i % n_s, 0, 0))
    out_shape = [
        jax.ShapeDtypeStruct((BATCH, SEQ, 2 * C_WIDTH), BF16),
        jax.ShapeDtypeStruct((BATCH, SEQ, C_WIDTH), BF16),
        jax.ShapeDtypeStruct((BATCH, SEQ // TK, C_WIDTH, TK), BF16),
    ]
    return pl.pallas_call(
        _proj_odd_kernel, grid=(BATCH * n_s,),
        in_specs=[row(D_MODEL), _resident(gmix.shape), _resident(wqk.shape), _resident(wvt.shape)],
        out_specs=[row(2 * C_WIDTH), row(C_WIDTH), vt], out_shape=out_shape,
        compiler_params=_params(1), name="proj_odd")(x3, gmix, wqk, wvt)


def _diag_masks(fn):
    r = lax.broadcasted_iota(jnp.int32, (TK, TQ), 0)
    c = lax.broadcasted_iota(jnp.int32, (TK, TQ), 1)
    return fn(r, c), fn(r + TK, c)


A_ACC = A_V_DIM + 16
B_ACC = B_V_DIM + 16
SUM_ROWS = 16


def _softmax_rest(s, vt, mask, tmax, c, ql, m_ref, acc_ref, rows):
    if mask is not None:
        s = jnp.where(mask, s, -jnp.inf)
    if tmax is None:
        tmax = jnp.max(s, axis=0, keepdims=True)
    row = slice(8 * c, 8 * c + 1)
    m = m_ref[row, ql]
    m_new = jnp.maximum(m, tmax)
    alpha = jnp.exp2(m - m_new)
    p = jnp.exp2(s - m_new)
    m_ref[row, ql] = m_new
    ones_row = lax.broadcasted_iota(jnp.int32, (SUM_ROWS, vt.shape[1]), 0) == 0
    vt_aug = jnp.concatenate([vt, jnp.where(ones_row, 1.0, 0.0).astype(BF16)], axis=0)
    acc_ref[rows, ql] = (alpha * acc_ref[rows, ql]
                         + jnp.dot(vt_aug, p.astype(BF16), preferred_element_type=F32))


ALL_Q = slice(0, TQ)
EARLY_Q = slice(0, TQ // 2)
LATE_Q = slice(TQ // 2, TQ)


def _softmax_attention(i, n, scores, rest, s_ref, t_ref, m_ref, acc_ref):
    acc_ref[...] = jnp.zeros_like(acc_ref)
    m_ref[...] = jnp.full_like(m_ref, NEG_BIG)

    def phase(j, slot, ql, mask, next_ql):
        for c in range(n):
            if next_ql is not None:
                nxt = scores(j + 1, c, next_ql)
                s_ref[1 - slot, c, :, next_ql] = nxt
                if next_ql == ALL_Q:
                    t_ref[1 - slot, 8 * c:8 * c + 1, :] = jnp.max(nxt, axis=0, keepdims=True)
            tmax = None if mask is not None else t_ref[slot, 8 * c:8 * c + 1, :]
            rest(j, c, ql, s_ref[slot, c, :, ql], None if mask is None else mask[:, ql], tmax)

    for c in range(n):
        first = scores(0, c, ALL_Q)
        s_ref[0, c] = first
        t_ref[0, 8 * c:8 * c + 1, :] = jnp.max(first, axis=0, keepdims=True)

    def body(u, carry):
        phase(2 * u, 0, ALL_Q, None, ALL_Q)
        phase(2 * u + 1, 1, ALL_Q, None, ALL_Q)
        return carry

    lax.fori_loop(0, i, body, 0)
    mask0, mask1 = _diag_masks(lambda kp, qp: (kp >> 6) <= (qp >> 6))
    phase(2 * i, 0, ALL_Q, mask0, LATE_Q)
    phase(2 * i + 1, 1, LATE_Q, mask1, None)


def _attn_a_kernel(q_ref, k_ref, vt_ref, lam_ref, gout_ref, o_ref, s_ref, t_ref, acc_ref, m_ref,
                   *, lam0):
    i = pl.program_id(1)

    def scores(j, c, ql):
        h = c // 2
        k = k_ref[pl.ds(pl.multiple_of(j * TK, TK), TK), LANES * h:LANES * (h + 1)]
        return lax.dot_general(k, q_ref[ql, LANES * c:LANES * (c + 1)], NT_DIMS,
                               preferred_element_type=F32)

    def rest(j, c, ql, s, mask, tmax):
        h = c // 2
        _softmax_rest(s, vt_ref[j, LANES * h:LANES * (h + 1), :], mask, tmax, c, ql, m_ref,
                      acc_ref, slice(A_ACC * c, A_ACC * (c + 1)))

    _softmax_attention(i, 2 * A_HEADS, scores, rest, s_ref, t_ref, m_ref, acc_ref)
    al = lam_ref[...]
    lam = (jnp.exp(jnp.sum(al[0:1] * al[1:2], axis=1, keepdims=True))
           - jnp.exp(jnp.sum(al[2:3] * al[3:4], axis=1, keepdims=True)) + lam0)
    for h in range(A_HEADS):
        outs = []
        for c in (2 * h, 2 * h + 1):
            num = acc_ref[A_ACC * c:A_ACC * c + A_V_DIM]
            den = acc_ref[A_ACC * c + A_V_DIM:A_ACC * c + A_V_DIM + 1]
            outs.append(num * (1.0 / den))
        o1, o2 = outs
        o = o1 - lam * o2
        o_ref[:, LANES * h:LANES * (h + 1)] = _rms(o.T, gout_ref[...]).astype(BF16)


def _kv_spec(n_lanes):
    return pl.BlockSpec((None, SEQ, n_lanes), lambda b, i: (b, 0, 0), pipeline_mode=pl.Buffered(1))


def _vt_spec(n_rows):
    return pl.BlockSpec((None, SEQ // TK, n_rows, TK), lambda b, i: (b, 0, 0, 0),
                        pipeline_mode=pl.Buffered(1))


def _attn_a(aq, ak, avt, a_lambda, gout, lam0):
    n_q = SEQ // TQ
    n_chain = 2 * A_HEADS
    kern = functools.partial(_attn_a_kernel, lam0=lam0)
    return pl.pallas_call(
        kern, grid=(BATCH, n_q),
        in_specs=[pl.BlockSpec((None, TQ, n_chain * LANES), lambda b, i: (b, i, 0)),
                  _kv_spec(512), _vt_spec(512),
                  _resident(a_lambda.shape), _resident(gout.shape)],
        out_specs=pl.BlockSpec((None, TQ, 512), lambda b, i: (b, i, 0)),
        out_shape=jax.ShapeDtypeStruct((BATCH, SEQ, 512), BF16),
        scratch_shapes=[pltpu.VMEM((2, n_chain, TK, TQ), F32), pltpu.VMEM((2, 8 * n_chain, TQ), F32),
                        pltpu.VMEM((n_chain * A_ACC, TQ), F32), pltpu.VMEM((8 * n_chain, TQ), F32)],
        compiler_params=_params(2), name="attn_a")(aq, ak, avt, a_lambda, gout)


def _attn_b_kernel(q_ref, k_ref, vt_ref, o_ref, s_ref, t_ref, acc_ref, m_ref):
    i = pl.program_id(1)

    def scores(j, h, ql):
        cols = slice(LANES * h, LANES * (h + 1))
        return lax.dot_general(k_ref[pl.ds(pl.multiple_of(j * TK, TK), TK), cols], q_ref[ql, cols],
                               NT_DIMS, preferred_element_type=F32)

    def rest(j, h, ql, s, mask, tmax):
        _softmax_rest(s, vt_ref[j, B_V_DIM * h:B_V_DIM * (h + 1), :], mask, tmax, h, ql, m_ref,
                      acc_ref, slice(B_ACC * h, B_ACC * (h + 1)))

    _softmax_attention(i, B_HEADS, scores, rest, s_ref, t_ref, m_ref, acc_ref)
    for hp in range(B_HEADS // 2):
        parts = []
        for h in (2 * hp, 2 * hp + 1):
            num = acc_ref[B_ACC * h:B_ACC * h + B_V_DIM]
            den = acc_ref[B_ACC * h + B_V_DIM:B_ACC * h + B_V_DIM + 1]
            parts.append(num * (1.0 / den))
        o_ref[:, LANES * hp:LANES * (hp + 1)] = jnp.concatenate(parts, axis=0).T.astype(BF16)


def _attn_b(bq, bk, bvt):
    n_q = SEQ // TQ
    return pl.pallas_call(
        _attn_b_kernel, grid=(BATCH, n_q),
        in_specs=[pl.BlockSpec((None, TQ, 1024), lambda b, i: (b, i, 0)),
                  _kv_spec(1024), _vt_spec(512)],
        out_specs=pl.BlockSpec((None, TQ, 512), lambda b, i: (b, i, 0)),
        out_shape=jax.ShapeDtypeStruct((BATCH, SEQ, 512), BF16),
        scratch_shapes=[pltpu.VMEM((2, B_HEADS, TK, TQ), F32), pltpu.VMEM((2, 8 * B_HEADS, TQ), F32),
                        pltpu.VMEM((B_HEADS * B_ACC, TQ), F32), pltpu.VMEM((8 * B_HEADS, TQ), F32)],
        compiler_params=_params(2), name="attn_b")(bq, bk, bvt)


def _stick_front(z, u_ref, mask):
    log1p2 = jnp.log(1.0 + jnp.exp2(jnp.minimum(z, 126.0))) * LOG2E
    sp = jnp.maximum(z, log1p2)
    logsig = z - sp
    if mask is not None:
        sp = jnp.where(mask, sp, 0.0)
    tl = jnp.dot(u_ref[...], sp.astype(BF16), preferred_element_type=F32)
    return logsig, tl[0:1, :] + sp[0:1, :], tl


def _stick_back(front, vt, mask, c, tail_ref, acc_ref, rows, ql):
    logsig, tile_mass, tl = front
    row = slice(8 * c, 8 * c + 1)
    tail = tail_ref[row, ql]
    w = jnp.exp2(logsig - tl)
    if mask is not None:
        w = jnp.where(mask, w, 0.0)
    pv = jnp.dot(vt, w.astype(BF16), preferred_element_type=F32)
    acc_ref[rows, ql] = acc_ref[rows, ql] + pv * jnp.exp2(-tail)
    tail_ref[row, ql] = tail + tile_mass


def _attn_c_kernel(q_ref, k_ref, vt_ref, u_ref, o_ref, s_ref, acc_ref, tail_ref):
    i = pl.program_id(1)
    n = C_HEADS

    def scores(j, h, ql):
        k = k_ref[pl.ds(pl.multiple_of(j * TK, TK), TK), LANES * (h // 2):LANES * (h // 2 + 1)]
        return lax.dot_general(k, q_ref[ql, LANES * h:LANES * (h + 1)], NT_DIMS,
                               preferred_element_type=F32)

    def phase(j, slot, ql, mask, j_next, next_ql=ALL_Q):
        fronts = {}
        qmask = None if mask is None else mask[:, ql]
        for c in range(n + STICK_LAG):
            if c < n:
                s_ref[1 - slot, c, :, next_ql] = scores(j_next, c, next_ql)
                fronts[c] = _stick_front(s_ref[slot, c, :, ql], u_ref, qmask)
            d = c - STICK_LAG
            if d >= 0:
                rows = slice(C_HEAD_DIM * d, C_HEAD_DIM * (d + 1))
                _stick_back(fronts.pop(d), vt_ref[j, rows, :], qmask, d, tail_ref, acc_ref,
                            rows, ql)

    acc_ref[...] = jnp.zeros_like(acc_ref)
    state_row = lax.broadcasted_iota(jnp.int32, tail_ref.shape, 0)
    tail_ref[...] = jnp.where((state_row & 7) == 0, 0.0, -NEG_BIG)
    for c in range(n):
        s_ref[0, c, :, LATE_Q] = scores(2 * i + 1, c, LATE_Q)
    mask0, mask1 = _diag_masks(lambda kp, qp: kp < qp)
    phase(2 * i + 1, 0, LATE_Q, mask1, 2 * i)
    phase(2 * i, 1, ALL_Q, mask0, jnp.maximum(2 * i - 1, 0))

    def min_tail(ql=ALL_Q):
        return jnp.min(tail_ref[:, ql])

    def live_phase(j, slot, j_next):
        late_alive = min_tail(LATE_Q) < DEAD_TAIL

        @pl.when(late_alive)
        def _():
            phase(j, slot, ALL_Q, None, j_next, ALL_Q)

        @pl.when(jnp.logical_not(late_alive))
        def _():
            phase(j, slot, EARLY_Q, None, j_next, EARLY_Q)

    def cond(carry):
        u, tmin = carry
        return jnp.logical_and(u < i, tmin < DEAD_TAIL)

    def body(carry):
        u, _ = carry
        j = 2 * i - 1 - 2 * u
        live_phase(j, 0, j - 1)

        @pl.when(min_tail() < DEAD_TAIL)
        def _():
            live_phase(j - 1, 1, jnp.maximum(j - 2, 0))

        return u + 1, min_tail()

    lax.while_loop(cond, body, (jnp.int32(0), min_tail()))
    for hp in range(C_HEADS // 2):
        blk = slice(LANES * hp, LANES * (hp + 1))
        o_ref[:, blk] = acc_ref[blk].T.astype(BF16)


def _attn_c(q, k, vt, u):
    n_q = SEQ // TQ
    return pl.pallas_call(
        _attn_c_kernel, grid=(BATCH, n_q),
        in_specs=[pl.BlockSpec((None, TQ, 2 * C_WIDTH), lambda b, i: (b, i, 0)),
                  _kv_spec(C_WIDTH), _vt_spec(C_WIDTH), _resident(u.shape)],
        out_specs=pl.BlockSpec((None, TQ, C_WIDTH), lambda b, i: (b, i, 0)),
        out_shape=jax.ShapeDtypeStruct((BATCH, SEQ, C_WIDTH), BF16),
        scratch_shapes=[pltpu.VMEM((2, C_HEADS, TK, TQ), F32), pltpu.VMEM((C_WIDTH, TQ), F32),
                        pltpu.VMEM((8 * C_HEADS, TQ), F32)],
        compiler_params=_params(2), name="attn_c")(q, k, vt, u)


def _ffn_kernel(*refs):
    x_ref, part_refs = refs[0], refs[1:-7]
    wo_ref, g_ref, wg_ref, wu_ref, wd_ref, o_ref, a_ref = refs[-7:]
    x = x_ref[...]
    off = 0
    for p_ref in part_refs:
        n = p_ref.shape[1]
        x = x + jnp.dot(p_ref[...], wo_ref[off:off + n, :], preferred_element_type=F32)
        off += n
    hb = _rms(x, g_ref[...]).astype(BF16)
    for c in range(D_FF // FF_CHUNK):
        sl = slice(c * FF_CHUNK, (c + 1) * FF_CHUNK)
        g = jnp.dot(hb, wg_ref[:, sl], preferred_element_type=F32)
        u = jnp.dot(hb, wu_ref[:, sl], preferred_element_type=F32)
        a_ref[:, sl] = (g * (1.0 / (1.0 + jnp.exp(-g))) * u).astype(BF16)
    o_ref[...] = x + jnp.dot(a_ref[...], wd_ref[...], preferred_element_type=F32)


def _outproj_ffn(x2, parts, w_out, gain, wg, wu, wd):
    t = x2.shape[0]
    row = lambda n: pl.BlockSpec((TM_FFN, n), lambda i: (i, 0))
    weights = [w_out, gain, wg, wu, wd]
    return pl.pallas_call(
        _ffn_kernel, grid=(t // TM_FFN,),
        in_specs=([row(D_MODEL)] + [row(p.shape[1]) for p in parts]
                  + [_resident(w.shape) for w in weights]),
        out_specs=row(D_MODEL), out_shape=jax.ShapeDtypeStruct(x2.shape, F32),
        scratch_shapes=[pltpu.VMEM((TM_FFN, D_FF), BF16)],
        compiler_params=_params(1), name="outproj_ffn")(x2, *parts, *weights)


def _consts():
    lane = np.arange(LANES)
    pos = jnp.arange(SEQ, dtype=F32)[:, None]

    def table(half, idx, active, neg):
        inv_freq = jnp.power(ROPE_THETA, -jnp.arange(half, dtype=F32) / half)
        ang = pos * inv_freq[idx][None, :]
        act = jnp.asarray(active)[None, :]
        sign = jnp.asarray(np.where(neg, -1.0, 1.0), F32)[None, :]
        return jnp.where(act, jnp.cos(ang), 1.0), jnp.where(act, jnp.sin(ang) * sign, 0.0)

    cosa, sina = table(32, lane % 32, np.ones(LANES, bool), (lane % 64) < 32)
    rope_b = (lane >= 64) & (lane < 96)
    cosb, sinb = table(16, (lane - 64) % 16, rope_b, lane < 80)

    i256 = np.arange(MXU_DIM)
    ga = ((i256[:, None] // 64) == (i256[None, :] // 64)) / 64.0
    blk = i256 // LANES
    w = i256 % LANES
    grp = np.where(w < 64, 0, np.where(w < 96, 1, 2))
    same = (blk[:, None] == blk[None, :]) & (grp[:, None] == grp[None, :])
    scale = np.where(grp == 0, 1.0 / 64, np.where(grp == 1, 1.0 / 32, 0.0))
    gb = same * scale[None, :]
    it = np.arange(TK)
    u = (it[None, :] > it[:, None]).astype(np.float32)
    return dict(cosa=cosa, sina=sina, cosb=cosb, sinb=sinb,
                ga=jnp.asarray(ga, BF16), gb=jnp.asarray(gb, BF16), u=jnp.asarray(u, BF16))


def _pad_heads(wm, n_heads, width):
    k = wm.shape[0]
    w3 = wm.reshape(k, n_heads, width)
    return jnp.pad(w3, ((0, 0), (0, 0), (0, LANES - width))).reshape(k, n_heads * LANES)


def _tile_gain(parts, reps, scale=1.0):
    g = jnp.concatenate([p.astype(F32) for p in parts])
    g = jnp.pad(g, (0, LANES - g.shape[0]))
    return (jnp.tile(g, reps) * scale)[None, :]


def _even_weights(i, l, norm_mix, ab_w_in, a_q_norm, a_k_norm, a_out_norm, b_q_a_norm, b_w_q_b,
                  b_kv_a_norm, b_w_kv_b, b_q_nope_norm, b_q_rope_norm, b_k_nope_norm,
                  b_k_rope_norm):
    w_in = ab_w_in[i]
    kr_cols = jnp.pad(w_in[:, 1920:1952], ((0, 0), (64, 32)))
    wtok = jnp.concatenate([w_in[:, 0:1024], w_in[:, 1536:1920], kr_cols], axis=1).astype(BF16)
    wkv = b_w_kv_b[i].reshape(B_KV_RANK, B_HEADS, B_NOPE_DIM + B_V_DIM)
    sa = (A_QK_DIM ** -0.5) * LOG2E
    sb = ((B_NOPE_DIM + B_ROPE_DIM) ** -0.5) * LOG2E
    zeros32 = jnp.zeros((32,), F32)
    return dict(
        gmix=norm_mix[l][None, :],
        wtok=wtok,
        wavt=w_in[:, 1024:1536].T.astype(BF16),
        wqb=_pad_heads(b_w_q_b[i], B_HEADS, B_NOPE_DIM + B_ROPE_DIM).astype(BF16),
        wkb=_pad_heads(wkv[:, :, :B_NOPE_DIM].reshape(B_KV_RANK, -1), B_HEADS,
                       B_NOPE_DIM).astype(BF16),
        wbvt=wkv[:, :, B_NOPE_DIM:].reshape(B_KV_RANK, -1).T.astype(BF16),
        gaq=_tile_gain([a_q_norm[i], a_q_norm[i]], 4, sa),
        gak=_tile_gain([a_k_norm[i], a_k_norm[i]], 4),
        gqa=b_q_a_norm[i][None, :],
        gkva=b_kv_a_norm[i][None, :],
        gbq=_tile_gain([b_q_nope_norm[i], b_q_rope_norm[i]], 8, sb),
        gbk=_tile_gain([b_k_nope_norm[i]], 8),
        gkr=_tile_gain([jnp.zeros((64,), F32), b_k_rope_norm[i], zeros32], 1),
        gout=a_out_norm[i][None, :] * (1.0 - _lambda_init(l)),
    )


def kernel(x, norm_mix, norm_ffn, ab_w_in, a_q_norm, a_k_norm, a_lambda, a_out_norm, b_q_a_norm,
           b_w_q_b, b_kv_a_norm, b_w_kv_b, b_q_nope_norm, b_q_rope_norm, b_k_nope_norm,
           b_k_rope_norm, ab_w_out, c_w_in, c_w_out, ffn_w_gate, ffn_w_up, ffn_w_down):
    consts = _consts()
    t = BATCH * SEQ
    for l in range(DEPTH):
        i = l // 2
        x3 = x.reshape(BATCH, SEQ, D_MODEL)
        if l % 2 == 0:
            w = _even_weights(i, l, norm_mix, ab_w_in, a_q_norm, a_k_norm, a_out_norm, b_q_a_norm,
                              b_w_q_b, b_kv_a_norm, b_w_kv_b, b_q_nope_norm, b_q_rope_norm,
                              b_k_nope_norm, b_k_rope_norm)
            aq, ak, avt, bq, bk, bvt = _proj_even(x3, consts, w)
            out_a = _attn_a(aq, ak, avt, a_lambda[i], w["gout"], _lambda_init(l))
            out_b = _attn_b(bq, bk, bvt)
            parts = [out_a.reshape(t, 512), out_b.reshape(t, 512)]
            w_out = ab_w_out[i].astype(BF16)
        else:
            w_in = c_w_in[i]
            q, k, vt = _proj_odd(x3, norm_mix[l][None, :], w_in[:, :2 * C_WIDTH].astype(BF16),
                                 w_in[:, 2 * C_WIDTH:].T.astype(BF16))
            out_c = _attn_c(q, k, vt, consts["u"])
            parts = [out_c.reshape(t, C_WIDTH)]
            w_out = c_w_out[i].astype(BF16)
        x2 = _outproj_ffn(x.reshape(t, D_MODEL), parts, w_out, norm_ffn[l][None, :],
                          ffn_w_gate[l].astype(BF16), ffn_w_up[l].astype(BF16),
                          ffn_w_down[l].astype(BF16))
        x = x2.reshape(BATCH, SEQ, D_MODEL)
    return x
```

```python
import functools
import math

import jax
import jax.numpy as jnp
import numpy as np
from jax import lax
from jax.experimental import pallas as pl
from jax.experimental.pallas import tpu as pltpu

D_MODEL = 1024
BATCH = 16
SEQ = 2048
DEPTH = 4
CHUNK = 64
ROPE_THETA = 10000.0
NORM_EPS = 1e-6

A_HEADS = 4
A_QK_DIM = 64
A_V_DIM = 128
B_HEADS = 8
B_Q_RANK = 256
B_KV_RANK = 128
B_NOPE_DIM = 64
B_ROPE_DIM = 32
B_V_DIM = 64
C_HEADS = 16
C_HEAD_DIM = 64
C_WIDTH = C_HEADS * C_HEAD_DIM
D_FF = 2816

LANES = 128
MXU_DIM = 256
TQ = 512
TK = 256
TM = 512
TM_FFN = 512
FF_CHUNK = 256
VMEM_LIMIT_BYTES = 56 * 1024 * 1024
LOG2E = 1.4426950408889634
NEG_BIG = -1e30
DEAD_TAIL = 160.0
STICK_LAG = 1

F32 = jnp.float32
BF16 = jnp.bfloat16
NT_DIMS = (((1,), (1,)), ((), ()))


def _lambda_init(layer_idx):
    return 0.8 - 0.6 * math.exp(-0.3 * layer_idx)


def _params(n_axes):
    return pltpu.CompilerParams(dimension_semantics=("arbitrary",) * n_axes,
                                vmem_limit_bytes=VMEM_LIMIT_BYTES)


def _resident(shape):
    zeros = (0,) * len(shape)
    return pl.BlockSpec(shape, lambda *_: zeros, pipeline_mode=pl.Buffered(1))


def _rms(x, gain):
    ms = jnp.mean(x * x, axis=1, keepdims=True)
    return x * lax.rsqrt(ms + NORM_EPS) * gain


def _group_mean_sq(y, g_ref):
    outs = []
    for c in range(y.shape[1] // MXU_DIM):
        yc = y[:, c * MXU_DIM:(c + 1) * MXU_DIM]
        outs.append(jnp.dot((yc * yc).astype(BF16), g_ref[...], preferred_element_type=F32))
    return jnp.concatenate(outs, axis=1)


def _group_rms(y, g_ref, gain):
    return y * lax.rsqrt(_group_mean_sq(y, g_ref) + NORM_EPS) * gain


def _rope(y, first_half, half, cos, sin_signed):
    n = y.shape[1]
    partner = jnp.where(first_half, pltpu.roll(y, n - half, 1), pltpu.roll(y, half, 1))
    return y * cos + partner * sin_signed


def _split_halves(q):
    lo_lanes = lax.broadcasted_iota(jnp.int32, (q.shape[0], LANES), 1) < 64
    blocks = []
    for b in range(q.shape[1] // LANES):
        blk = q[:, b * LANES:(b + 1) * LANES]
        blocks += [jnp.where(lo_lanes, blk, 0.0), jnp.where(lo_lanes, 0.0, blk)]
    return jnp.concatenate(blocks, axis=1)


def _proj_even_kernel(x_ref, gmix_ref, wtok_ref, wavt_ref, wqb_ref, wkb_ref, wbvt_ref,
                      ga_ref, gb_ref, gaq_ref, gak_ref, gqa_ref, gkva_ref, gbq_ref, gbk_ref,
                      gkr_ref, cosa_ref, sina_ref, cosb_ref, sinb_ref,
                      aq_ref, ak_ref, avt_ref, bq_ref, bk_ref, bvt_ref):
    tm = x_ref.shape[0]
    hb = _rms(x_ref[...], gmix_ref[...]).astype(BF16)
    p = jnp.dot(hb, wtok_ref[...], preferred_element_type=F32)

    lane_a = lax.broadcasted_iota(jnp.int32, (tm, 512), 1)
    first_a = (lane_a & 32) == 0
    cosa = jnp.concatenate([cosa_ref[...]] * 4, axis=1)
    sina = jnp.concatenate([sina_ref[...]] * 4, axis=1)
    aq = _rope(_group_rms(p[:, 0:512], ga_ref, gaq_ref[...]), first_a, 32, cosa, sina)
    aq_ref[...] = _split_halves(aq).astype(BF16)
    ak = _rope(_group_rms(p[:, 512:1024], ga_ref, gak_ref[...]), first_a, 32, cosa, sina)
    ak_ref[...] = ak.astype(BF16)
    avt = lax.dot_general(wavt_ref[...], hb, NT_DIMS, preferred_element_type=F32)
    for c in range(tm // TK):
        avt_ref[c] = avt[:, c * TK:(c + 1) * TK].astype(BF16)

    qln = _rms(p[:, 1024:1280], gqa_ref[...]).astype(BF16)
    qb = jnp.dot(qln, wqb_ref[...], preferred_element_type=F32)
    lane_b = lax.broadcasted_iota(jnp.int32, (tm, 1024), 1)
    first_b = (lane_b & 127) < 80
    cosb = jnp.concatenate([cosb_ref[...]] * 8, axis=1)
    sinb = jnp.concatenate([sinb_ref[...]] * 8, axis=1)
    bq = _rope(_group_rms(qb, gb_ref, gbq_ref[...]), first_b, 16, cosb, sinb)
    bq_ref[...] = bq.astype(BF16)

    kvn = _rms(p[:, 1280:1408], gkva_ref[...]).astype(BF16)
    kb = _group_rms(jnp.dot(kvn, wkb_ref[...], preferred_element_type=F32),
                    gb_ref, gbk_ref[...])
    kr = p[:, 1408:1536]
    krn = kr * lax.rsqrt(jnp.sum(kr * kr, axis=1, keepdims=True) * (1.0 / B_ROPE_DIM)
                         + NORM_EPS) * gkr_ref[...]
    lane_r = lax.broadcasted_iota(jnp.int32, (tm, 128), 1)
    krr = _rope(krn, lane_r < 80, 16, cosb_ref[...], sinb_ref[...])
    bk_ref[...] = (kb + jnp.concatenate([krr] * 8, axis=1)).astype(BF16)
    bvt = lax.dot_general(wbvt_ref[...], kvn, NT_DIMS, preferred_element_type=F32)
    for c in range(tm // TK):
        bvt_ref[c] = bvt[:, c * TK:(c + 1) * TK].astype(BF16)


def _proj_even(x3, consts, w):
    n_s = SEQ // TM
    grid = (BATCH * n_s,)
    row = lambda n: pl.BlockSpec((None, TM, n), lambda i: (i // n_s, i % n_s, 0))
    vt = lambda n: pl.BlockSpec((None, TM // TK, n, TK), lambda i: (i // n_s, i % n_s, 0, 0))
    tab = pl.BlockSpec((TM, LANES), lambda i: (i % n_s, 0))
    ins = [x3, w["gmix"], w["wtok"], w["wavt"], w["wqb"], w["wkb"], w["wbvt"],
           consts["ga"], consts["gb"], w["gaq"], w["gak"], w["gqa"], w["gkva"], w["gbq"],
           w["gbk"], w["gkr"], consts["cosa"], consts["sina"], consts["cosb"], consts["sinb"]]
    in_specs = [row(D_MODEL)] + [_resident(a.shape) for a in ins[1:16]] + [tab] * 4
    out_shape = [
        jax.ShapeDtypeStruct((BATCH, SEQ, 1024), BF16),
        jax.ShapeDtypeStruct((BATCH, SEQ, 512), BF16),
        jax.ShapeDtypeStruct((BATCH, SEQ // TK, 512, TK), BF16),
        jax.ShapeDtypeStruct((BATCH, SEQ, 1024), BF16),
        jax.ShapeDtypeStruct((BATCH, SEQ, 1024), BF16),
        jax.ShapeDtypeStruct((BATCH, SEQ // TK, 512, TK), BF16),
    ]
    out_specs = [row(1024), row(512), vt(512), row(1024), row(1024), vt(512)]
    return pl.pallas_call(_proj_even_kernel, grid=grid, in_specs=in_specs, out_specs=out_specs,
                          out_shape=out_shape, compiler_params=_params(1),
                          name="proj_even")(*ins)


def _proj_odd_kernel(x_ref, gmix_ref, wqk_ref, wvt_ref, q_ref, k_ref, vt_ref):
    tm = x_ref.shape[0]
    hb = _rms(x_ref[...], gmix_ref[...]).astype(BF16)
    qk = jnp.dot(hb, wqk_ref[...], preferred_element_type=F32)
    q_ref[...] = _split_halves(qk[:, :C_WIDTH] * (C_HEAD_DIM ** -0.5 * LOG2E)).astype(BF16)
    k_ref[...] = qk[:, C_WIDTH:].astype(BF16)
    vt = lax.dot_general(wvt_ref[...], hb, NT_DIMS, preferred_element_type=F32)
    for c in range(tm // TK):
        vt_ref[c] = vt[:, c * TK:(c + 1) * TK].astype(BF16)


def _proj_odd(x3, gmix, wqk, wvt):
    n_s = SEQ // TM
    row = lambda n: pl.BlockSpec((None, TM, n), lambda i: (i // n_s, i % n_s, 0))
    vt = pl.BlockSpec((None, TM // TK, C_WIDTH, TK), lambda i: (i // n_s, i % n_s, 0, 0))
    out_shape = [
        jax.ShapeDtypeStruct((BATCH, SEQ, 2 * C_WIDTH), BF16),
        jax.ShapeDtypeStruct((BATCH, SEQ, C_WIDTH), BF16),
        jax.ShapeDtypeStruct((BATCH, SEQ // TK, C_WIDTH, TK), BF16),
    ]
    return pl.pallas_call(
        _proj_odd_kernel, grid=(BATCH * n_s,),
        in_specs=[row(D_MODEL), _resident(gmix.shape), _resident(wqk.shape), _resident(wvt.shape)],
        out_specs=[row(2 * C_WIDTH), row(C_WIDTH), vt], out_shape=out_shape,
        compiler_params=_params(1), name="proj_odd")(x3, gmix, wqk, wvt)


def _diag_masks(fn):
    r = lax.broadcasted_iota(jnp.int32, (TK, TQ), 0)
    c = lax.broadcasted_iota(jnp.int32, (TK, TQ), 1)
    return fn(r, c), fn(r + TK, c)


A_ACC = A_V_DIM + 16
B_ACC = B_V_DIM + 16
SUM_ROWS = 16


def _softmax_rest(s, vt, mask, tmax, c, ql, m_ref, acc_ref, rows):
    if mask is not None:
        s = jnp.where(mask, s, -jnp.inf)
    if tmax is None:
        tmax = jnp.max(s, axis=0, keepdims=True)
    row = slice(8 * c, 8 * c + 1)
    m = m_ref[row, ql]
    m_new = jnp.maximum(m, tmax)
    alpha = jnp.exp2(m - m_new)
    p = jnp.exp2(s - m_new)
    m_ref[row, ql] = m_new
    ones_row = lax.broadcasted_iota(jnp.int32, (SUM_ROWS, vt.shape[1]), 0) == 0
    vt_aug = jnp.concatenate([vt, jnp.where(ones_row, 1.0, 0.0).astype(BF16)], axis=0)
    acc_ref[rows, ql] = (alpha * acc_ref[rows, ql]
                         + jnp.dot(vt_aug, p.astype(BF16), preferred_element_type=F32))


ALL_Q = slice(0, TQ)
EARLY_Q = slice(0, TQ // 2)
LATE_Q = slice(TQ // 2, TQ)


def _softmax_attention(i, n, scores, rest, s_ref, t_ref, m_ref, acc_ref):
    acc_ref[...] = jnp.zeros_like(acc_ref)
    m_ref[...] = jnp.full_like(m_ref, NEG_BIG)

    def phase(j, slot, ql, mask, next_ql):
        for c in range(n):
            if next_ql is not None:
                nxt = scores(j + 1, c, next_ql)
                s_ref[1 - slot, c, :, next_ql] = nxt
                if next_ql == ALL_Q:
                    t_ref[1 - slot, 8 * c:8 * c + 1, :] = jnp.max(nxt, axis=0, keepdims=True)
            tmax = None if mask is not None else t_ref[slot, 8 * c:8 * c + 1, :]
            rest(j, c, ql, s_ref[slot, c, :, ql], None if mask is None else mask[:, ql], tmax)

    for c in range(n):
        first = scores(0, c, ALL_Q)
        s_ref[0, c] = first
        t_ref[0, 8 * c:8 * c + 1, :] = jnp.max(first, axis=0, keepdims=True)

    def body(u, carry):
        phase(2 * u, 0, ALL_Q, None, ALL_Q)
        phase(2 * u + 1, 1, ALL_Q, None, ALL_Q)
        return carry

    lax.fori_loop(0, i, body, 0)
    mask0, mask1 = _diag_masks(lambda kp, qp: (kp >> 6) <= (qp >> 6))
    phase(2 * i, 0, ALL_Q, mask0, LATE_Q)
    phase(2 * i + 1, 1, LATE_Q, mask1, None)


def _attn_a_kernel(q_ref, k_ref, vt_ref, lam_ref, gout_ref, o_ref, s_ref, t_ref, acc_ref, m_ref,
                   *, lam0):
    i = pl.program_id(1)

    def scores(j, c, ql):
        h = c // 2
        k = k_ref[pl.ds(pl.multiple_of(j * TK, TK), TK), LANES * h:LANES * (h + 1)]
        return lax.dot_general(k, q_ref[ql, LANES * c:LANES * (c + 1)], NT_DIMS,
                               preferred_element_type=F32)

    def rest(j, c, ql, s, mask, tmax):
        h = c // 2
        _softmax_rest(s, vt_ref[j, LANES * h:LANES * (h + 1), :], mask, tmax, c, ql, m_ref,
                      acc_ref, slice(A_ACC * c, A_ACC * (c + 1)))

    _softmax_attention(i, 2 * A_HEADS, scores, rest, s_ref, t_ref, m_ref, acc_ref)
    al = lam_ref[...]
    lam = (jnp.exp(jnp.sum(al[0:1] * al[1:2], axis=1, keepdims=True))
           - jnp.exp(jnp.sum(al[2:3] * al[3:4], axis=1, keepdims=True)) + lam0)
    for h in range(A_HEADS):
        outs = []
        for c in (2 * h, 2 * h + 1):
            num = acc_ref[A_ACC * c:A_ACC * c + A_V_DIM]
            den = acc_ref[A_ACC * c + A_V_DIM:A_ACC * c + A_V_DIM + 1]
            outs.append(num * (1.0 / den))
        o1, o2 = outs
        o = o1 - lam * o2
        o_ref[:, LANES * h:LANES * (h + 1)] = _rms(o.T, gout_ref[...]).astype(BF16)


def _kv_spec(n_lanes):
    return pl.BlockSpec((None, SEQ, n_lanes), lambda b, i: (b, 0, 0))


def _vt_spec(n_rows):
    return pl.BlockSpec((None, SEQ // TK, n_rows, TK), lambda b, i: (b, 0, 0, 0))


def _attn_a(aq, ak, avt, a_lambda, gout, lam0):
    n_q = SEQ // TQ
    n_chain = 2 * A_HEADS
    kern = functools.partial(_attn_a_kernel, lam0=lam0)
    return pl.pallas_call(
        kern, grid=(BATCH, n_q),
        in_specs=[pl.BlockSpec((None, TQ, n_chain * LANES), lambda b, i: (b, i, 0)),
                  _kv_spec(512), _vt_spec(512),
                  _resident(a_lambda.shape), _resident(gout.shape)],
        out_specs=pl.BlockSpec((None, TQ, 512), lambda b, i: (b, i, 0)),
        out_shape=jax.ShapeDtypeStruct((BATCH, SEQ, 512), BF16),
        scratch_shapes=[pltpu.VMEM((2, n_chain, TK, TQ), F32), pltpu.VMEM((2, 8 * n_chain, TQ), F32),
                        pltpu.VMEM((n_chain * A_ACC, TQ), F32), pltpu.VMEM((8 * n_chain, TQ), F32)],
        compiler_params=_params(2), name="attn_a")(aq, ak, avt, a_lambda, gout)


def _attn_b_kernel(q_ref, k_ref, vt_ref, o_ref, s_ref, t_ref, acc_ref, m_ref):
    i = pl.program_id(1)

    def scores(j, h, ql):
        cols = slice(LANES * h, LANES * (h + 1))
        return lax.dot_general(k_ref[pl.ds(pl.multiple_of(j * TK, TK), TK), cols], q_ref[ql, cols],
                               NT_DIMS, preferred_element_type=F32)

    def rest(j, h, ql, s, mask, tmax):
        _softmax_rest(s, vt_ref[j, B_V_DIM * h:B_V_DIM * (h + 1), :], mask, tmax, h, ql, m_ref,
                      acc_ref, slice(B_ACC * h, B_ACC * (h + 1)))

    _softmax_attention(i, B_HEADS, scores, rest, s_ref, t_ref, m_ref, acc_ref)
    for hp in range(B_HEADS // 2):
        parts = []
        for h in (2 * hp, 2 * hp + 1):
            num = acc_ref[B_ACC * h:B_ACC * h + B_V_DIM]
            den = acc_ref[B_ACC * h + B_V_DIM:B_ACC * h + B_V_DIM + 1]
            parts.append(num * (1.0 / den))
        o_ref[:, LANES * hp:LANES * (hp + 1)] = jnp.concatenate(parts, axis=0).T.astype(BF16)


def _attn_b(bq, bk, bvt):
    n_q = SEQ // TQ
    return pl.pallas_call(
        _attn_b_kernel, grid=(BATCH, n_q),
        in_specs=[pl.BlockSpec((None, TQ, 1024), lambda b, i: (b, i, 0)),
                  _kv_spec(1024), _vt_spec(512)],
        out_specs=pl.BlockSpec((None, TQ, 512), lambda b, i: (b, i, 0)),
        out_shape=jax.ShapeDtypeStruct((BATCH, SEQ, 512), BF16),
        scratch_shapes=[pltpu.VMEM((2, B_HEADS, TK, TQ), F32), pltpu.VMEM((2, 8 * B_HEADS, TQ), F32),
                        pltpu.VMEM((B_HEADS * B_ACC, TQ), F32), pltpu.VMEM((8 * B_HEADS, TQ), F32)],
        compiler_params=_params(2), name="attn_b")(bq, bk, bvt)


def _stick_front(z, u_ref, mask):
    log1p2 = jnp.log(1.0 + jnp.exp2(jnp.minimum(z, 126.0))) * LOG2E
    sp = jnp.maximum(z, log1p2)
    logsig = z - sp
    if mask is not None:
        sp = jnp.where(mask, sp, 0.0)
    tl = jnp.dot(u_ref[...], sp.astype(BF16), preferred_element_type=F32)
    return logsig, tl[0:1, :] + sp[0:1, :], tl


def _stick_back(front, vt, mask, c, tail_ref, acc_ref, rows, ql):
    logsig, tile_mass, tl = front
    row = slice(8 * c, 8 * c + 1)
    tail = tail_ref[row, ql]
    w = jnp.exp2(logsig - tl)
    if mask is not None:
        w = jnp.where(mask, w, 0.0)
    pv = jnp.dot(vt, w.astype(BF16), preferred_element_type=F32)
    acc_ref[rows, ql] = acc_ref[rows, ql] + pv * jnp.exp2(-tail)
    tail_ref[row, ql] = tail + tile_mass


def _attn_c_kernel(q_ref, k_ref, vt_ref, u_ref, o_ref, s_ref, acc_ref, tail_ref):
    i = pl.program_id(1)
    n = C_HEADS

    def scores(j, h, ql):
        k = k_ref[pl.ds(pl.multiple_of(j * TK, TK), TK), LANES * (h // 2):LANES * (h // 2 + 1)]
        return lax.dot_general(k, q_ref[ql, LANES * h:LANES * (h + 1)], NT_DIMS,
                               preferred_element_type=F32)

    def phase(j, slot, ql, mask, j_next, next_ql=ALL_Q):
        fronts = {}
        qmask = None if mask is None else mask[:, ql]
        for c in range(n + STICK_LAG):
            if c < n:
                s_ref[1 - slot, c, :, next_ql] = scores(j_next, c, next_ql)
                fronts[c] = _stick_front(s_ref[slot, c, :, ql], u_ref, qmask)
            d = c - STICK_LAG
            if d >= 0:
                rows = slice(C_HEAD_DIM * d, C_HEAD_DIM * (d + 1))
                _stick_back(fronts.pop(d), vt_ref[j, rows, :], qmask, d, tail_ref, acc_ref,
                            rows, ql)

    acc_ref[...] = jnp.zeros_like(acc_ref)
    state_row = lax.broadcasted_iota(jnp.int32, tail_ref.shape, 0)
    tail_ref[...] = jnp.where((state_row & 7) == 0, 0.0, -NEG_BIG)
    for c in range(n):
        s_ref[0, c, :, LATE_Q] = scores(2 * i + 1, c, LATE_Q)
    mask0, mask1 = _diag_masks(lambda kp, qp: kp < qp)
    phase(2 * i + 1, 0, LATE_Q, mask1, 2 * i)
    phase(2 * i, 1, ALL_Q, mask0, jnp.maximum(2 * i - 1, 0))

    def min_tail(ql=ALL_Q):
        return jnp.min(tail_ref[:, ql])

    def live_phase(j, slot, j_next):
        late_alive = min_tail(LATE_Q) < DEAD_TAIL

        @pl.when(late_alive)
        def _():
            phase(j, slot, ALL_Q, None, j_next, ALL_Q)

        @pl.when(jnp.logical_not(late_alive))
        def _():
            phase(j, slot, EARLY_Q, None, j_next, EARLY_Q)

    def cond(carry):
        u, tmin = carry
        return jnp.logical_and(u < i, tmin < DEAD_TAIL)

    def body(carry):
        u, _ = carry
        j = 2 * i - 1 - 2 * u
        live_phase(j, 0, j - 1)

        @pl.when(min_tail() < DEAD_TAIL)
        def _():
            live_phase(j - 1, 1, jnp.maximum(j - 2, 0))

        return u + 1, min_tail()

    lax.while_loop(cond, body, (jnp.int32(0), min_tail()))
    for hp in range(C_HEADS // 2):
        blk = slice(LANES * hp, LANES * (hp + 1))
        o_ref[:, blk] = acc_ref[blk].T.astype(BF16)


def _attn_c(q, k, vt, u):
    n_q = SEQ // TQ
    return pl.pallas_call(
        _attn_c_kernel, grid=(BATCH, n_q),
        in_specs=[pl.BlockSpec((None, TQ, 2 * C_WIDTH), lambda b, i: (b, i, 0)),
                  _kv_spec(C_WIDTH), _vt_spec(C_WIDTH), _resident(u.shape)],
        out_specs=pl.BlockSpec((None, TQ, C_WIDTH), lambda b, i: (b, i, 0)),
        out_shape=jax.ShapeDtypeStruct((BATCH, SEQ, C_WIDTH), BF16),
        scratch_shapes=[pltpu.VMEM((2, C_HEADS, TK, TQ), F32), pltpu.VMEM((C_WIDTH, TQ), F32),
                        pltpu.VMEM((8 * C_HEADS, TQ), F32)],
        compiler_params=_params(2), name="attn_c")(q, k, vt, u)


def _ffn_kernel(*refs):
    x_ref, part_refs = refs[0], refs[1:-7]
    wo_ref, g_ref, wg_ref, wu_ref, wd_ref, o_ref, a_ref = refs[-7:]
    x = x_ref[...]
    off = 0
    for p_ref in part_refs:
        n = p_ref.shape[1]
        x = x + jnp.dot(p_ref[...], wo_ref[off:off + n, :], preferred_element_type=F32)
        off += n
    hb = _rms(x, g_ref[...]).astype(BF16)
    for c in range(D_FF // FF_CHUNK):
        sl = slice(c * FF_CHUNK, (c + 1) * FF_CHUNK)
        g = jnp.dot(hb, wg_ref[:, sl], preferred_element_type=F32)
        u = jnp.dot(hb, wu_ref[:, sl], preferred_element_type=F32)
        a_ref[:, sl] = (g * (1.0 / (1.0 + jnp.exp(-g))) * u).astype(BF16)
    o_ref[...] = x + jnp.dot(a_ref[...], wd_ref[...], preferred_element_type=F32)


def _outproj_ffn(x2, parts, w_out, gain, wg, wu, wd):
    t = x2.shape[0]
    row = lambda n: pl.BlockSpec((TM_FFN, n), lambda i: (i, 0))
    weights = [w_out, gain, wg, wu, wd]
    return pl.pallas_call(
        _ffn_kernel, grid=(t // TM_FFN,),
        in_specs=([row(D_MODEL)] + [row(p.shape[1]) for p in parts]
                  + [_resident(w.shape) for w in weights]),
        out_specs=row(D_MODEL), out_shape=jax.ShapeDtypeStruct(x2.shape, F32),
        scratch_shapes=[pltpu.VMEM((TM_FFN, D_FF), BF16)],
        compiler_params=_params(1), name="outproj_ffn")(x2, *parts, *weights)


def _consts():
    lane = np.arange(LANES)
    pos = jnp.arange(SEQ, dtype=F32)[:, None]

    def table(half, idx, active, neg):
        inv_freq = jnp.power(ROPE_THETA, -jnp.arange(half, dtype=F32) / half)
        ang = pos * inv_freq[idx][None, :]
        act = jnp.asarray(active)[None, :]
        sign = jnp.asarray(np.where(neg, -1.0, 1.0), F32)[None, :]
        return jnp.where(act, jnp.cos(ang), 1.0), jnp.where(act, jnp.sin(ang) * sign, 0.0)

    cosa, sina = table(32, lane % 32, np.ones(LANES, bool), (lane % 64) < 32)
    rope_b = (lane >= 64) & (lane < 96)
    cosb, sinb = table(16, (lane - 64) % 16, rope_b, lane < 80)

    i256 = np.arange(MXU_DIM)
    ga = ((i256[:, None] // 64) == (i256[None, :] // 64)) / 64.0
    blk = i256 // LANES
    w = i256 % LANES
    grp = np.where(w < 64, 0, np.where(w < 96, 1, 2))
    same = (blk[:, None] == blk[None, :]) & (grp[:, None] == grp[None, :])
    scale = np.where(grp == 0, 1.0 / 64, np.where(grp == 1, 1.0 / 32, 0.0))
    gb = same * scale[None, :]
    it = np.arange(TK)
    u = (it[None, :] > it[:, None]).astype(np.float32)
    return dict(cosa=cosa, sina=sina, cosb=cosb, sinb=sinb,
                ga=jnp.asarray(ga, BF16), gb=jnp.asarray(gb, BF16), u=jnp.asarray(u, BF16))


def _pad_heads(wm, n_heads, width):
    k = wm.shape[0]
    w3 = wm.reshape(k, n_heads, width)
    return jnp.pad(w3, ((0, 0), (0, 0), (0, LANES - width))).reshape(k, n_heads * LANES)


def _tile_gain(parts, reps, scale=1.0):
    g = jnp.concatenate([p.astype(F32) for p in parts])
    g = jnp.pad(g, (0, LANES - g.shape[0]))
    return (jnp.tile(g, reps) * scale)[None, :]


def _even_weights(i, l, norm_mix, ab_w_in, a_q_norm, a_k_norm, a_out_norm, b_q_a_norm, b_w_q_b,
                  b_kv_a_norm, b_w_kv_b, b_q_nope_norm, b_q_rope_norm, b_k_nope_norm,
                  b_k_rope_norm):
    w_in = ab_w_in[i]
    kr_cols = jnp.pad(w_in[:, 1920:1952], ((0, 0), (64, 32)))
    wtok = jnp.concatenate([w_in[:, 0:1024], w_in[:, 1536:1920], kr_cols], axis=1).astype(BF16)
    wkv = b_w_kv_b[i].reshape(B_KV_RANK, B_HEADS, B_NOPE_DIM + B_V_DIM)
    sa = (A_QK_DIM ** -0.5) * LOG2E
    sb = ((B_NOPE_DIM + B_ROPE_DIM) ** -0.5) * LOG2E
    zeros32 = jnp.zeros((32,), F32)
    return dict(
        gmix=norm_mix[l][None, :],
        wtok=wtok,
        wavt=w_in[:, 1024:1536].T.astype(BF16),
        wqb=_pad_heads(b_w_q_b[i], B_HEADS, B_NOPE_DIM + B_ROPE_DIM).astype(BF16),
        wkb=_pad_heads(wkv[:, :, :B_NOPE_DIM].reshape(B_KV_RANK, -1), B_HEADS,
                       B_NOPE_DIM).astype(BF16),
        wbvt=wkv[:, :, B_NOPE_DIM:].reshape(B_KV_RANK, -1).T.astype(BF16),
        gaq=_tile_gain([a_q_norm[i], a_q_norm[i]], 4, sa),
        gak=_tile_gain([a_k_norm[i], a_k_norm[i]], 4),
        gqa=b_q_a_norm[i][None, :],
        gkva=b_kv_a_norm[i][None, :],
        gbq=_tile_gain([b_q_nope_norm[i], b_q_rope_norm[i]], 8, sb),
        gbk=_tile_gain([b_k_nope_norm[i]], 8),
        gkr=_tile_gain([jnp.zeros((64,), F32), b_k_rope_norm[i], zeros32], 1),
        gout=a_out_norm[i][None, :] * (1.0 - _lambda_init(l)),
    )


def kernel(x, norm_mix, norm_ffn, ab_w_in, a_q_norm, a_k_norm, a_lambda, a_out_norm, b_q_a_norm,
           b_w_q_b, b_kv_a_norm, b_w_kv_b, b_q_nope_norm, b_q_rope_norm, b_k_nope_norm,
           b_k_rope_norm, ab_w_out, c_w_in, c_w_out, ffn_w_gate, ffn_w_up, ffn_w_down):
    consts = _consts()
    t = BATCH * SEQ
    for l in range(DEPTH):
        i = l // 2
        x3 = x.reshape(BATCH, SEQ, D_MODEL)
        if l % 2 == 0:
            w = _even_weights(i, l, norm_mix, ab_w_in, a_q_norm, a_k_norm, a_out_norm, b_q_a_norm,
                              b_w_q_b, b_kv_a_norm, b_w_kv_b, b_q_nope_norm, b_q_rope_norm,
                              b_k_nope_norm, b_k_rope_norm)
            aq, ak, avt, bq, bk, bvt = _proj_even(x3, consts, w)
            out_a = _attn_a(aq, ak, avt, a_lambda[i], w["gout"], _lambda_init(l))
            out_b = _attn_b(bq, bk, bvt)
            parts = [out_a.reshape(t, 512), out_b.reshape(t, 512)]
            w_out = ab_w_out[i].astype(BF16)
        else:
            w_in = c_w_in[i]
            q, k, vt = _proj_odd(x3, norm_mix[l][None, :], w_in[:, :2 * C_WIDTH].astype(BF16),
                                 w_in[:, 2 * C_WIDTH:].T.astype(BF16))
            out_c = _attn_c(q, k, vt, consts["u"])
            parts = [out_c.reshape(t, C_WIDTH)]
            w_out = c_w_out[i].astype(BF16)
        x2 = _outproj_ffn(x.reshape(t, D_MODEL), parts, w_out, norm_ffn[l][None, :],
                          ffn_w_gate[l].astype(BF16), ffn_w_up[l].astype(BF16),
                          ffn_w_down[l].astype(BF16))
        x = x2.reshape(BATCH, SEQ, D_MODEL)
    return x
```

```python
import functools
import math

import jax
import jax.numpy as jnp
import numpy as np
from jax import lax
from jax.experimental import pallas as pl
from jax.experimental.pallas import tpu as pltpu

D_MODEL = 1024
BATCH = 16
SEQ = 2048
DEPTH = 4
CHUNK = 64
ROPE_THETA = 10000.0
NORM_EPS = 1e-6

A_HEADS = 4
A_QK_DIM = 64
A_V_DIM = 128
B_HEADS = 8
B_Q_RANK = 256
B_KV_RANK = 128
B_NOPE_DIM = 64
B_ROPE_DIM = 32
B_V_DIM = 64
C_HEADS = 16
C_HEAD_DIM = 64
C_WIDTH = C_HEADS * C_HEAD_DIM
D_FF = 2816

LANES = 128
MXU_DIM = 256
TQ = 512
TK = 256
TM = 512
TM_FFN = 512
FF_CHUNK = 256
VMEM_LIMIT_BYTES = 56 * 1024 * 1024
LOG2E = 1.4426950408889634
NEG_BIG = -1e30
DEAD_TAIL = 151.0
STICK_LAG = 1

F32 = jnp.float32
BF16 = jnp.bfloat16
NT_DIMS = (((1,), (1,)), ((), ()))


def _lambda_init(layer_idx):
    return 0.8 - 0.6 * math.exp(-0.3 * layer_idx)


def _params(n_axes):
    return pltpu.CompilerParams(dimension_semantics=("arbitrary",) * n_axes,
                                vmem_limit_bytes=VMEM_LIMIT_BYTES)


def _resident(shape):
    zeros = (0,) * len(shape)
    return pl.BlockSpec(shape, lambda *_: zeros, pipeline_mode=pl.Buffered(1))


def _rms(x, gain):
    ms = jnp.mean(x * x, axis=1, keepdims=True)
    return x * lax.rsqrt(ms + NORM_EPS) * gain


def _group_mean_sq(y, g_ref):
    outs = []
    for c in range(y.shape[1] // MXU_DIM):
        yc = y[:, c * MXU_DIM:(c + 1) * MXU_DIM]
        outs.append(jnp.dot((yc * yc).astype(BF16), g_ref[...], preferred_element_type=F32))
    return jnp.concatenate(outs, axis=1)


def _group_rms(y, g_ref, gain):
    return y * lax.rsqrt(_group_mean_sq(y, g_ref) + NORM_EPS) * gain


def _rope(y, first_half, half, cos, sin_signed):
    n = y.shape[1]
    partner = jnp.where(first_half, pltpu.roll(y, n - half, 1), pltpu.roll(y, half, 1))
    return y * cos + partner * sin_signed


def _split_halves(q):
    lo_lanes = lax.broadcasted_iota(jnp.int32, (q.shape[0], LANES), 1) < 64
    blocks = []
    for b in range(q.shape[1] // LANES):
        blk = q[:, b * LANES:(b + 1) * LANES]
        blocks += [jnp.where(lo_lanes, blk, 0.0), jnp.where(lo_lanes, 0.0, blk)]
    return jnp.concatenate(blocks, axis=1)


def _proj_even_kernel(x_ref, gmix_ref, wtok_ref, wavt_ref, wqb_ref, wkb_ref, wbvt_ref,
                      ga_ref, gb_ref, gaq_ref, gak_ref, gqa_ref, gkva_ref, gbq_ref, gbk_ref,
                      gkr_ref, cosa_ref, sina_ref, cosb_ref, sinb_ref,
                      aq_ref, ak_ref, avt_ref, bq_ref, bk_ref, bvt_ref):
    tm = x_ref.shape[0]
    hb = _rms(x_ref[...], gmix_ref[...]).astype(BF16)
    p = jnp.dot(hb, wtok_ref[...], preferred_element_type=F32)

    lane_a = lax.broadcasted_iota(jnp.int32, (tm, 512), 1)
    first_a = (lane_a & 32) == 0
    cosa = jnp.concatenate([cosa_ref[...]] * 4, axis=1)
    sina = jnp.concatenate([sina_ref[...]] * 4, axis=1)
    aq = _rope(_group_rms(p[:, 0:512], ga_ref, gaq_ref[...]), first_a, 32, cosa, sina)
    aq_ref[...] = _split_halves(aq).astype(BF16)
    ak = _rope(_group_rms(p[:, 512:1024], ga_ref, gak_ref[...]), first_a, 32, cosa, sina)
    ak_ref[...] = ak.astype(BF16)
    avt = lax.dot_general(wavt_ref[...], hb, NT_DIMS, preferred_element_type=F32)
    for c in range(tm // TK):
        avt_ref[c] = avt[:, c * TK:(c + 1) * TK].astype(BF16)

    qln = _rms(p[:, 1024:1280], gqa_ref[...]).astype(BF16)
    qb = jnp.dot(qln, wqb_ref[...], preferred_element_type=F32)
    lane_b = lax.broadcasted_iota(jnp.int32, (tm, 1024), 1)
    first_b = (lane_b & 127) < 80
    cosb = jnp.concatenate([cosb_ref[...]] * 8, axis=1)
    sinb = jnp.concatenate([sinb_ref[...]] * 8, axis=1)
    bq = _rope(_group_rms(qb, gb_ref, gbq_ref[...]), first_b, 16, cosb, sinb)
    bq_ref[...] = bq.astype(BF16)

    kvn = _rms(p[:, 1280:1408], gkva_ref[...]).astype(BF16)
    kb = _group_rms(jnp.dot(kvn, wkb_ref[...], preferred_element_type=F32),
                    gb_ref, gbk_ref[...])
    kr = p[:, 1408:1536]
    krn = kr * lax.rsqrt(jnp.sum(kr * kr, axis=1, keepdims=True) * (1.0 / B_ROPE_DIM)
                         + NORM_EPS) * gkr_ref[...]
    lane_r = lax.broadcasted_iota(jnp.int32, (tm, 128), 1)
    krr = _rope(krn, lane_r < 80, 16, cosb_ref[...], sinb_ref[...])
    bk_ref[...] = (kb + jnp.concatenate([krr] * 8, axis=1)).astype(BF16)
    bvt = lax.dot_general(wbvt_ref[...], kvn, NT_DIMS, preferred_element_type=F32)
    for c in range(tm // TK):
        bvt_ref[c] = bvt[:, c * TK:(c + 1) * TK].astype(BF16)


def _proj_even(x3, consts, w):
    n_s = SEQ // TM
    grid = (BATCH * n_s,)
    row = lambda n: pl.BlockSpec((None, TM, n), lambda i: (i // n_s, i % n_s, 0))
    vt = lambda n: pl.BlockSpec((None, TM // TK, n, TK), lambda i: (i // n_s, i % n_s, 0, 0))
    tab = pl.BlockSpec((TM, LANES), lambda i: (i % n_s, 0))
    ins = [x3, w["gmix"], w["wtok"], w["wavt"], w["wqb"], w["wkb"], w["wbvt"],
           consts["ga"], consts["gb"], w["gaq"], w["gak"], w["gqa"], w["gkva"], w["gbq"],
           w["gbk"], w["gkr"], consts["cosa"], consts["sina"], consts["cosb"], consts["sinb"]]
    in_specs = [row(D_MODEL)] + [_resident(a.shape) for a in ins[1:16]] + [tab] * 4
    out_shape = [
        jax.ShapeDtypeStruct((BATCH, SEQ, 1024), BF16),
        jax.ShapeDtypeStruct((BATCH, SEQ, 512), BF16),
        jax.ShapeDtypeStruct((BATCH, SEQ // TK, 512, TK), BF16),
        jax.ShapeDtypeStruct((BATCH, SEQ, 1024), BF16),
        jax.ShapeDtypeStruct((BATCH, SEQ, 1024), BF16),
        jax.ShapeDtypeStruct((BATCH, SEQ // TK, 512, TK), BF16),
    ]
    out_specs = [row(1024), row(512), vt(512), row(1024), row(1024), vt(512)]
    return pl.pallas_call(_proj_even_kernel, grid=grid, in_specs=in_specs, out_specs=out_specs,
                          out_shape=out_shape, compiler_params=_params(1),
                          name="proj_even")(*ins)


def _proj_odd_kernel(x_ref, gmix_ref, wqk_ref, wvt_ref, q_ref, k_ref, vt_ref):
    tm = x_ref.shape[0]
    hb = _rms(x_ref[...], gmix_ref[...]).astype(BF16)
    qk = jnp.dot(hb, wqk_ref[...], preferred_element_type=F32)
    q_ref[...] = _split_halves(qk[:, :C_WIDTH] * (C_HEAD_DIM ** -0.5 * LOG2E)).astype(BF16)
    k_ref[...] = qk[:, C_WIDTH:].astype(BF16)
    vt = lax.dot_general(wvt_ref[...], hb, NT_DIMS, preferred_element_type=F32)
    for c in range(tm // TK):
        vt_ref[c] = vt[:, c * TK:(c + 1) * TK].astype(BF16)


def _proj_odd(x3, gmix, wqk, wvt):
    n_s = SEQ // TM
    row = lambda n: pl.BlockSpec((None, TM, n), lambda i: (i // n_s, i % n_s, 0))
    vt = pl.BlockSpec((None, TM // TK, C_WIDTH, TK), lambda i: (i // n_s, i % n_s, 0, 0))
    out_shape = [
        jax.ShapeDtypeStruct((BATCH, SEQ, 2 * C_WIDTH), BF16),
        jax.ShapeDtypeStruct((BATCH, SEQ, C_WIDTH), BF16),
        jax.ShapeDtypeStruct((BATCH, SEQ // TK, C_WIDTH, TK), BF16),
    ]
    return pl.pallas_call(
        _proj_odd_kernel, grid=(BATCH * n_s,),
        in_specs=[row(D_MODEL), _resident(gmix.shape), _resident(wqk.shape), _resident(wvt.shape)],
        out_specs=[row(2 * C_WIDTH), row(C_WIDTH), vt], out_shape=out_shape,
        compiler_params=_params(1), name="proj_odd")(x3, gmix, wqk, wvt)


def _diag_masks(fn):
    r = lax.broadcasted_iota(jnp.int32, (TK, TQ), 0)
    c = lax.broadcasted_iota(jnp.int32, (TK, TQ), 1)
    return fn(r, c), fn(r + TK, c)


A_ACC = A_V_DIM + 16
B_ACC = B_V_DIM + 16
SUM_ROWS = 16


def _softmax_rest(s, vt, mask, tmax, c, ql, m_ref, acc_ref, rows):
    if mask is not None:
        s = jnp.where(mask, s, -jnp.inf)
    if tmax is None:
        tmax = jnp.max(s, axis=0, keepdims=True)
    row = slice(8 * c, 8 * c + 1)
    m = m_ref[row, ql]
    m_new = jnp.maximum(m, tmax)
    alpha = jnp.exp2(m - m_new)
    p = jnp.exp2(s - m_new)
    m_ref[row, ql] = m_new
    ones_row = lax.broadcasted_iota(jnp.int32, (SUM_ROWS, vt.shape[1]), 0) == 0
    vt_aug = jnp.concatenate([vt, jnp.where(ones_row, 1.0, 0.0).astype(BF16)], axis=0)
    acc_ref[rows, ql] = (alpha * acc_ref[rows, ql]
                         + jnp.dot(vt_aug, p.astype(BF16), preferred_element_type=F32))


ALL_Q = slice(0, TQ)
EARLY_Q = slice(0, TQ // 2)
LATE_Q = slice(TQ // 2, TQ)


def _softmax_attention(i, n, scores, rest, s_ref, t_ref, m_ref, acc_ref):
    acc_ref[...] = jnp.zeros_like(acc_ref)
    m_ref[...] = jnp.full_like(m_ref, NEG_BIG)

    def phase(j, slot, ql, mask, next_ql):
        for c in range(n):
            if next_ql is not None:
                nxt = scores(j + 1, c, next_ql)
                s_ref[1 - slot, c, :, next_ql] = nxt
                if next_ql == ALL_Q:
                    t_ref[1 - slot, 8 * c:8 * c + 1, :] = jnp.max(nxt, axis=0, keepdims=True)
            tmax = None if mask is not None else t_ref[slot, 8 * c:8 * c + 1, :]
            rest(j, c, ql, s_ref[slot, c, :, ql], None if mask is None else mask[:, ql], tmax)

    for c in range(n):
        first = scores(0, c, ALL_Q)
        s_ref[0, c] = first
        t_ref[0, 8 * c:8 * c + 1, :] = jnp.max(first, axis=0, keepdims=True)

    def body(u, carry):
        phase(2 * u, 0, ALL_Q, None, ALL_Q)
        phase(2 * u + 1, 1, ALL_Q, None, ALL_Q)
        return carry

    lax.fori_loop(0, i, body, 0)
    mask0, mask1 = _diag_masks(lambda kp, qp: (kp >> 6) <= (qp >> 6))
    phase(2 * i, 0, ALL_Q, mask0, LATE_Q)
    phase(2 * i + 1, 1, LATE_Q, mask1, None)


def _attn_a_kernel(q_ref, k_ref, vt_ref, lam_ref, gout_ref, o_ref, s_ref, t_ref, acc_ref, m_ref,
                   *, lam0):
    i = pl.program_id(1)

    def scores(j, c, ql):
        h = c // 2
        k = k_ref[pl.ds(pl.multiple_of(j * TK, TK), TK), LANES * h:LANES * (h + 1)]
        return lax.dot_general(k, q_ref[ql, LANES * c:LANES * (c + 1)], NT_DIMS,
                               preferred_element_type=F32)

    def rest(j, c, ql, s, mask, tmax):
        h = c // 2
        _softmax_rest(s, vt_ref[j, LANES * h:LANES * (h + 1), :], mask, tmax, c, ql, m_ref,
                      acc_ref, slice(A_ACC * c, A_ACC * (c + 1)))

    _softmax_attention(i, 2 * A_HEADS, scores, rest, s_ref, t_ref, m_ref, acc_ref)
    al = lam_ref[...]
    lam = (jnp.exp(jnp.sum(al[0:1] * al[1:2], axis=1, keepdims=True))
           - jnp.exp(jnp.sum(al[2:3] * al[3:4], axis=1, keepdims=True)) + lam0)
    for h in range(A_HEADS):
        outs = []
        for c in (2 * h, 2 * h + 1):
            num = acc_ref[A_ACC * c:A_ACC * c + A_V_DIM]
            den = acc_ref[A_ACC * c + A_V_DIM:A_ACC * c + A_V_DIM + 1]
            outs.append(num * (1.0 / den))
        o1, o2 = outs
        o = o1 - lam * o2
        o_ref[:, LANES * h:LANES * (h + 1)] = _rms(o.T, gout_ref[...]).astype(BF16)


def _kv_spec(n_lanes):
    return pl.BlockSpec((None, SEQ, n_lanes), lambda b, i: (b, 0, 0))


def _vt_spec(n_rows):
    return pl.BlockSpec((None, SEQ // TK, n_rows, TK), lambda b, i: (b, 0, 0, 0))


def _attn_a(aq, ak, avt, a_lambda, gout, lam0):
    n_q = SEQ // TQ
    n_chain = 2 * A_HEADS
    kern = functools.partial(_attn_a_kernel, lam0=lam0)
    return pl.pallas_call(
        kern, grid=(BATCH, n_q),
        in_specs=[pl.BlockSpec((None, TQ, n_chain * LANES), lambda b, i: (b, i, 0)),
                  _kv_spec(512), _vt_spec(512),
                  _resident(a_lambda.shape), _resident(gout.shape)],
        out_specs=pl.BlockSpec((None, TQ, 512), lambda b, i: (b, i, 0)),
        out_shape=jax.ShapeDtypeStruct((BATCH, SEQ, 512), BF16),
        scratch_shapes=[pltpu.VMEM((2, n_chain, TK, TQ), F32), pltpu.VMEM((2, 8 * n_chain, TQ), F32),
                        pltpu.VMEM((n_chain * A_ACC, TQ), F32), pltpu.VMEM((8 * n_chain, TQ), F32)],
        compiler_params=_params(2), name="attn_a")(aq, ak, avt, a_lambda, gout)


def _attn_b_kernel(q_ref, k_ref, vt_ref, o_ref, s_ref, t_ref, acc_ref, m_ref):
    i = pl.program_id(1)

    def scores(j, h, ql):
        cols = slice(LANES * h, LANES * (h + 1))
        return lax.dot_general(k_ref[pl.ds(pl.multiple_of(j * TK, TK), TK), cols], q_ref[ql, cols],
                               NT_DIMS, preferred_element_type=F32)

    def rest(j, h, ql, s, mask, tmax):
        _softmax_rest(s, vt_ref[j, B_V_DIM * h:B_V_DIM * (h + 1), :], mask, tmax, h, ql, m_ref,
                      acc_ref, slice(B_ACC * h, B_ACC * (h + 1)))

    _softmax_attention(i, B_HEADS, scores, rest, s_ref, t_ref, m_ref, acc_ref)
    for hp in range(B_HEADS // 2):
        parts = []
        for h in (2 * hp, 2 * hp + 1):
            num = acc_ref[B_ACC * h:B_ACC * h + B_V_DIM]
            den = acc_ref[B_ACC * h + B_V_DIM:B_ACC * h + B_V_DIM + 1]
            parts.append(num * (1.0 / den))
        o_ref[:, LANES * hp:LANES * (hp + 1)] = jnp.concatenate(parts, axis=0).T.astype(BF16)


def _attn_b(bq, bk, bvt):
    n_q = SEQ // TQ
    return pl.pallas_call(
        _attn_b_kernel, grid=(BATCH, n_q),
        in_specs=[pl.BlockSpec((None, TQ, 1024), lambda b, i: (b, i, 0)),
                  _kv_spec(1024), _vt_spec(512)],
        out_specs=pl.BlockSpec((None, TQ, 512), lambda b, i: (b, i, 0)),
        out_shape=jax.ShapeDtypeStruct((BATCH, SEQ, 512), BF16),
        scratch_shapes=[pltpu.VMEM((2, B_HEADS, TK, TQ), F32), pltpu.VMEM((2, 8 * B_HEADS, TQ), F32),
                        pltpu.VMEM((B_HEADS * B_ACC, TQ), F32), pltpu.VMEM((8 * B_HEADS, TQ), F32)],
        compiler_params=_params(2), name="attn_b")(bq, bk, bvt)


def _stick_front(z, u_ref, mask):
    log1p2 = jnp.log(1.0 + jnp.exp2(jnp.minimum(z, 126.0))) * LOG2E
    sp = jnp.maximum(z, log1p2)
    logsig = z - sp
    if mask is not None:
        sp = jnp.where(mask, sp, 0.0)
    tl = jnp.dot(u_ref[...], sp.astype(BF16), preferred_element_type=F32)
    return logsig, tl[0:1, :] + sp[0:1, :], tl


def _stick_back(front, vt, mask, c, tail_ref, acc_ref, rows, ql):
    logsig, tile_mass, tl = front
    row = slice(8 * c, 8 * c + 1)
    tail = tail_ref[row, ql]
    w = jnp.exp2(logsig - tl)
    if mask is not None:
        w = jnp.where(mask, w, 0.0)
    pv = jnp.dot(vt, w.astype(BF16), preferred_element_type=F32)
    acc_ref[rows, ql] = acc_ref[rows, ql] + pv * jnp.exp2(-tail)
    tail_ref[row, ql] = tail + tile_mass


def _attn_c_kernel(q_ref, k_ref, vt_ref, u_ref, o_ref, s_ref, acc_ref, tail_ref):
    i = pl.program_id(1)
    n = C_HEADS

    def scores(j, h, ql):
        k = k_ref[pl.ds(pl.multiple_of(j * TK, TK), TK), LANES * (h // 2):LANES * (h // 2 + 1)]
        return lax.dot_general(k, q_ref[ql, LANES * h:LANES * (h + 1)], NT_DIMS,
                               preferred_element_type=F32)

    def phase(j, slot, ql, mask, j_next, next_ql=ALL_Q):
        fronts = {}
        qmask = None if mask is None else mask[:, ql]
        for c in range(n + STICK_LAG):
            if c < n:
                s_ref[1 - slot, c, :, next_ql] = scores(j_next, c, next_ql)
                fronts[c] = _stick_front(s_ref[slot, c, :, ql], u_ref, qmask)
            d = c - STICK_LAG
            if d >= 0:
                rows = slice(C_HEAD_DIM * d, C_HEAD_DIM * (d + 1))
                _stick_back(fronts.pop(d), vt_ref[j, rows, :], qmask, d, tail_ref, acc_ref,
                            rows, ql)

    acc_ref[...] = jnp.zeros_like(acc_ref)
    state_row = lax.broadcasted_iota(jnp.int32, tail_ref.shape, 0)
    tail_ref[...] = jnp.where((state_row & 7) == 0, 0.0, -NEG_BIG)
    for c in range(n):
        s_ref[0, c, :, LATE_Q] = scores(2 * i + 1, c, LATE_Q)
    mask0, mask1 = _diag_masks(lambda kp, qp: kp < qp)
    phase(2 * i + 1, 0, LATE_Q, mask1, 2 * i)
    phase(2 * i, 1, ALL_Q, mask0, jnp.maximum(2 * i - 1, 0))

    def min_tail(ql=ALL_Q):
        return jnp.min(tail_ref[:, ql])

    def live_phase(j, slot, j_next):
        late_alive = min_tail(LATE_Q) < DEAD_TAIL

        @pl.when(late_alive)
        def _():
            phase(j, slot, ALL_Q, None, j_next, ALL_Q)

        @pl.when(jnp.logical_not(late_alive))
        def _():
            phase(j, slot, EARLY_Q, None, j_next, EARLY_Q)

    def cond(carry):
        u, tmin = carry
        return jnp.logical_and(u < i, tmin < DEAD_TAIL)

    def body(carry):
        u, _ = carry
        j = 2 * i - 1 - 2 * u
        live_phase(j, 0, j - 1)

        @pl.when(min_tail() < DEAD_TAIL)
        def _():
            live_phase(j - 1, 1, jnp.maximum(j - 2, 0))

        return u + 1, min_tail()

    lax.while_loop(cond, body, (jnp.int32(0), min_tail()))
    for hp in range(C_HEADS // 2):
        blk = slice(LANES * hp, LANES * (hp + 1))
        o_ref[:, blk] = acc_ref[blk].T.astype(BF16)


def _attn_c(q, k, vt, u):
    n_q = SEQ // TQ
    return pl.pallas_call(
        _attn_c_kernel, grid=(BATCH, n_q),
        in_specs=[pl.BlockSpec((None, TQ, 2 * C_WIDTH), lambda b, i: (b, i, 0)),
                  _kv_spec(C_WIDTH), _vt_spec(C_WIDTH), _resident(u.shape)],
        out_specs=pl.BlockSpec((None, TQ, C_WIDTH), lambda b, i: (b, i, 0)),
        out_shape=jax.ShapeDtypeStruct((BATCH, SEQ, C_WIDTH), BF16),
        scratch_shapes=[pltpu.VMEM((2, C_HEADS, TK, TQ), F32), pltpu.VMEM((C_WIDTH, TQ), F32),
                        pltpu.VMEM((8 * C_HEADS, TQ), F32)],
        compiler_params=_params(2), name="attn_c")(q, k, vt, u)


def _ffn_kernel(*refs):
    x_ref, part_refs = refs[0], refs[1:-7]
    wo_ref, g_ref, wg_ref, wu_ref, wd_ref, o_ref, a_ref = refs[-7:]
    x = x_ref[...]
    off = 0
    for p_ref in part_refs:
        n = p_ref.shape[1]
        x = x + jnp.dot(p_ref[...], wo_ref[off:off + n, :], preferred_element_type=F32)
        off += n
    hb = _rms(x, g_ref[...]).astype(BF16)
    for c in range(D_FF // FF_CHUNK):
        sl = slice(c * FF_CHUNK, (c + 1) * FF_CHUNK)
        g = jnp.dot(hb, wg_ref[:, sl], preferred_element_type=F32)
        u = jnp.dot(hb, wu_ref[:, sl], preferred_element_type=F32)
        a_ref[:, sl] = (g * (1.0 / (1.0 + jnp.exp(-g))) * u).astype(BF16)
    o_ref[...] = x + jnp.dot(a_ref[...], wd_ref[...], preferred_element_type=F32)


def _outproj_ffn(x2, parts, w_out, gain, wg, wu, wd):
    t = x2.shape[0]
    row = lambda n: pl.BlockSpec((TM_FFN, n), lambda i: (i, 0))
    weights = [w_out, gain, wg, wu, wd]
    return pl.pallas_call(
        _ffn_kernel, grid=(t // TM_FFN,),
        in_specs=([row(D_MODEL)] + [row(p.shape[1]) for p in parts]
                  + [_resident(w.shape) for w in weights]),
        out_specs=row(D_MODEL), out_shape=jax.ShapeDtypeStruct(x2.shape, F32),
        scratch_shapes=[pltpu.VMEM((TM_FFN, D_FF), BF16)],
        compiler_params=_params(1), name="outproj_ffn")(x2, *parts, *weights)


def _consts():
    lane = np.arange(LANES)
    pos = jnp.arange(SEQ, dtype=F32)[:, None]

    def table(half, idx, active, neg):
        inv_freq = jnp.power(ROPE_THETA, -jnp.arange(half, dtype=F32) / half)
        ang = pos * inv_freq[idx][None, :]
        act = jnp.asarray(active)[None, :]
        sign = jnp.asarray(np.where(neg, -1.0, 1.0), F32)[None, :]
        return jnp.where(act, jnp.cos(ang), 1.0), jnp.where(act, jnp.sin(ang) * sign, 0.0)

    cosa, sina = table(32, lane % 32, np.ones(LANES, bool), (lane % 64) < 32)
    rope_b = (lane >= 64) & (lane < 96)
    cosb, sinb = table(16, (lane - 64) % 16, rope_b, lane < 80)

    i256 = np.arange(MXU_DIM)
    ga = ((i256[:, None] // 64) == (i256[None, :] // 64)) / 64.0
    blk = i256 // LANES
    w = i256 % LANES
    grp = np.where(w < 64, 0, np.where(w < 96, 1, 2))
    same = (blk[:, None] == blk[None, :]) & (grp[:, None] == grp[None, :])
    scale = np.where(grp == 0, 1.0 / 64, np.where(grp == 1, 1.0 / 32, 0.0))
    gb = same * scale[None, :]
    it = np.arange(TK)
    u = (it[None, :] > it[:, None]).astype(np.float32)
    return dict(cosa=cosa, sina=sina, cosb=cosb, sinb=sinb,
                ga=jnp.asarray(ga, BF16), gb=jnp.asarray(gb, BF16), u=jnp.asarray(u, BF16))


def _pad_heads(wm, n_heads, width):
    k = wm.shape[0]
    w3 = wm.reshape(k, n_heads, width)
    return jnp.pad(w3, ((0, 0), (0, 0), (0, LANES - width))).reshape(k, n_heads * LANES)


def _tile_gain(parts, reps, scale=1.0):
    g = jnp.concatenate([p.astype(F32) for p in parts])
    g = jnp.pad(g, (0, LANES - g.shape[0]))
    return (jnp.tile(g, reps) * scale)[None, :]


def _even_weights(i, l, norm_mix, ab_w_in, a_q_norm, a_k_norm, a_out_norm, b_q_a_norm, b_w_q_b,
                  b_kv_a_norm, b_w_kv_b, b_q_nope_norm, b_q_rope_norm, b_k_nope_norm,
                  b_k_rope_norm):
    w_in = ab_w_in[i]
    kr_cols = jnp.pad(w_in[:, 1920:1952], ((0, 0), (64, 32)))
    wtok = jnp.concatenate([w_in[:, 0:1024], w_in[:, 1536:1920], kr_cols], axis=1).astype(BF16)
    wkv = b_w_kv_b[i].reshape(B_KV_RANK, B_HEADS, B_NOPE_DIM + B_V_DIM)
    sa = (A_QK_DIM ** -0.5) * LOG2E
    sb = ((B_NOPE_DIM + B_ROPE_DIM) ** -0.5) * LOG2E
    zeros32 = jnp.zeros((32,), F32)
    return dict(
        gmix=norm_mix[l][None, :],
        wtok=wtok,
        wavt=w_in[:, 1024:1536].T.astype(BF16),
        wqb=_pad_heads(b_w_q_b[i], B_HEADS, B_NOPE_DIM + B_ROPE_DIM).astype(BF16),
        wkb=_pad_heads(wkv[:, :, :B_NOPE_DIM].reshape(B_KV_RANK, -1), B_HEADS,
                       B_NOPE_DIM).astype(BF16),
        wbvt=wkv[:, :, B_NOPE_DIM:].reshape(B_KV_RANK, -1).T.astype(BF16),
        gaq=_tile_gain([a_q_norm[i], a_q_norm[i]], 4, sa),
        gak=_tile_gain([a_k_norm[i], a_k_norm[i]], 4),
        gqa=b_q_a_norm[i][None, :],
        gkva=b_kv_a_norm[i][None, :],
        gbq=_tile_gain([b_q_nope_norm[i], b_q_rope_norm[i]], 8, sb),
        gbk=_tile_gain([b_k_nope_norm[i]], 8),
        gkr=_tile_gain([jnp.zeros((64,), F32), b_k_rope_norm[i], zeros32], 1),
        gout=a_out_norm[i][None, :] * (1.0 - _lambda_init(l)),
    )


def kernel(x, norm_mix, norm_ffn, ab_w_in, a_q_norm, a_k_norm, a_lambda, a_out_norm, b_q_a_norm,
           b_w_q_b, b_kv_a_norm, b_w_kv_b, b_q_nope_norm, b_q_rope_norm, b_k_nope_norm,
           b_k_rope_norm, ab_w_out, c_w_in, c_w_out, ffn_w_gate, ffn_w_up, ffn_w_down):
    consts = _consts()
    t = BATCH * SEQ
    for l in range(DEPTH):
        i = l // 2
        x3 = x.reshape(BATCH, SEQ, D_MODEL)
        if l % 2 == 0:
            w = _even_weights(i, l, norm_mix, ab_w_in, a_q_norm, a_k_norm, a_out_norm, b_q_a_norm,
                              b_w_q_b, b_kv_a_norm, b_w_kv_b, b_q_nope_norm, b_q_rope_norm,
                              b_k_nope_norm, b_k_rope_norm)
            aq, ak, avt, bq, bk, bvt = _proj_even(x3, consts, w)
            out_a = _attn_a(aq, ak, avt, a_lambda[i], w["gout"], _lambda_init(l))
            out_b = _attn_b(bq, bk, bvt)
            parts = [out_a.reshape(t, 512), out_b.reshape(t, 512)]
            w_out = ab_w_out[i].astype(BF16)
        else:
            w_in = c_w_in[i]
            q, k, vt = _proj_odd(x3, norm_mix[l][None, :], w_in[:, :2 * C_WIDTH].astype(BF16),
                                 w_in[:, 2 * C_WIDTH:].T.astype(BF16))
            out_c = _attn_c(q, k, vt, consts["u"])
            parts = [out_c.reshape(t, C_WIDTH)]
            w_out = c_w_out[i].astype(BF16)
        x2 = _outproj_ffn(x.reshape(t, D_MODEL), parts, w_out, norm_ffn[l][None, :],
                          ffn_w_gate[l].astype(BF16), ffn_w_up[l].astype(BF16),
                          ffn_w_down[l].astype(BF16))
        x = x2.reshape(BATCH, SEQ, D_MODEL)
    return x
```

```python
import functools
import math

import jax
import jax.numpy as jnp
import numpy as np
from jax import lax
from jax.experimental import pallas as pl
from jax.experimental.pallas import tpu as pltpu

D_MODEL = 1024
BATCH = 16
SEQ = 2048
DEPTH = 4
CHUNK = 64
ROPE_THETA = 10000.0
NORM_EPS = 1e-6

A_HEADS = 4
A_QK_DIM = 64
A_V_DIM = 128
B_HEADS = 8
B_Q_RANK = 256
B_KV_RANK = 128
B_NOPE_DIM = 64
B_ROPE_DIM = 32
B_V_DIM = 64
C_HEADS = 16
C_HEAD_DIM = 64
C_WIDTH = C_HEADS * C_HEAD_DIM
D_FF = 2816

LANES = 128
MXU_DIM = 256
TQ = 512
TK = 256
TM = 512
TM_FFN = 1024
FF_CHUNK = 256
VMEM_LIMIT_BYTES = 56 * 1024 * 1024
LOG2E = 1.4426950408889634
NEG_BIG = -1e30
DEAD_TAIL = 151.0
STICK_LAG = 1

F32 = jnp.float32
BF16 = jnp.bfloat16
NT_DIMS = (((1,), (1,)), ((), ()))


def _lambda_init(layer_idx):
    return 0.8 - 0.6 * math.exp(-0.3 * layer_idx)


def _params(n_axes):
    return pltpu.CompilerParams(dimension_semantics=("arbitrary",) * n_axes,
                                vmem_limit_bytes=VMEM_LIMIT_BYTES)


def _resident(shape):
    zeros = (0,) * len(shape)
    return pl.BlockSpec(shape, lambda *_: zeros, pipeline_mode=pl.Buffered(1))


def _rms(x, gain):
    ms = jnp.mean(x * x, axis=1, keepdims=True)
    return x * lax.rsqrt(ms + NORM_EPS) * gain


def _group_mean_sq(y, g_ref):
    outs = []
    for c in range(y.shape[1] // MXU_DIM):
        yc = y[:, c * MXU_DIM:(c + 1) * MXU_DIM]
        outs.append(jnp.dot((yc * yc).astype(BF16), g_ref[...], preferred_element_type=F32))
    return jnp.concatenate(outs, axis=1)


def _group_rms(y, g_ref, gain):
    return y * lax.rsqrt(_group_mean_sq(y, g_ref) + NORM_EPS) * gain


def _rope(y, first_half, half, cos, sin_signed):
    n = y.shape[1]
    partner = jnp.where(first_half, pltpu.roll(y, n - half, 1), pltpu.roll(y, half, 1))
    return y * cos + partner * sin_signed


def _split_halves(q):
    lo_lanes = lax.broadcasted_iota(jnp.int32, (q.shape[0], LANES), 1) < 64
    blocks = []
    for b in range(q.shape[1] // LANES):
        blk = q[:, b * LANES:(b + 1) * LANES]
        blocks += [jnp.where(lo_lanes, blk, 0.0), jnp.where(lo_lanes, 0.0, blk)]
    return jnp.concatenate(blocks, axis=1)


def _proj_even_kernel(x_ref, gmix_ref, wtok_ref, wavt_ref, wqb_ref, wkb_ref, wbvt_ref,
                      ga_ref, gb_ref, gaq_ref, gak_ref, gqa_ref, gkva_ref, gbq_ref, gbk_ref,
                      gkr_ref, cosa_ref, sina_ref, cosb_ref, sinb_ref,
                      aq_ref, ak_ref, avt_ref, bq_ref, bk_ref, bvt_ref):
    tm = x_ref.shape[0]
    hb = _rms(x_ref[...], gmix_ref[...]).astype(BF16)
    p = jnp.dot(hb, wtok_ref[...], preferred_element_type=F32)

    lane_a = lax.broadcasted_iota(jnp.int32, (tm, 512), 1)
    first_a = (lane_a & 32) == 0
    cosa = jnp.concatenate([cosa_ref[...]] * 4, axis=1)
    sina = jnp.concatenate([sina_ref[...]] * 4, axis=1)
    aq = _rope(_group_rms(p[:, 0:512], ga_ref, gaq_ref[...]), first_a, 32, cosa, sina)
    aq_ref[...] = _split_halves(aq).astype(BF16)
    ak = _rope(_group_rms(p[:, 512:1024], ga_ref, gak_ref[...]), first_a, 32, cosa, sina)
    ak_ref[...] = ak.astype(BF16)
    avt = lax.dot_general(wavt_ref[...], hb, NT_DIMS, preferred_element_type=F32)
    for c in range(tm // TK):
        avt_ref[c] = avt[:, c * TK:(c + 1) * TK].astype(BF16)

    qln = _rms(p[:, 1024:1280], gqa_ref[...]).astype(BF16)
    qb = jnp.dot(qln, wqb_ref[...], preferred_element_type=F32)
    lane_b = lax.broadcasted_iota(jnp.int32, (tm, 1024), 1)
    first_b = (lane_b & 127) < 80
    cosb = jnp.concatenate([cosb_ref[...]] * 8, axis=1)
    sinb = jnp.concatenate([sinb_ref[...]] * 8, axis=1)
    bq = _rope(_group_rms(qb, gb_ref, gbq_ref[...]), first_b, 16, cosb, sinb)
    bq_ref[...] = bq.astype(BF16)

    kvn = _rms(p[:, 1280:1408], gkva_ref[...]).astype(BF16)
    kb = _group_rms(jnp.dot(kvn, wkb_ref[...], preferred_element_type=F32),
                    gb_ref, gbk_ref[...])
    kr = p[:, 1408:1536]
    krn = kr * lax.rsqrt(jnp.sum(kr * kr, axis=1, keepdims=True) * (1.0 / B_ROPE_DIM)
                         + NORM_EPS) * gkr_ref[...]
    lane_r = lax.broadcasted_iota(jnp.int32, (tm, 128), 1)
    krr = _rope(krn, lane_r < 80, 16, cosb_ref[...], sinb_ref[...])
    bk_ref[...] = (kb + jnp.concatenate([krr] * 8, axis=1)).astype(BF16)
    bvt = lax.dot_general(wbvt_ref[...], kvn, NT_DIMS, preferred_element_type=F32)
    for c in range(tm // TK):
        bvt_ref[c] = bvt[:, c * TK:(c + 1) * TK].astype(BF16)


def _proj_even(x3, consts, w):
    n_s = SEQ // TM
    grid = (BATCH * n_s,)
    row = lambda n: pl.BlockSpec((None, TM, n), lambda i: (i // n_s, i % n_s, 0))
    vt = lambda n: pl.BlockSpec((None, TM // TK, n, TK), lambda i: (i // n_s, i % n_s, 0, 0))
    tab = pl.BlockSpec((TM, LANES), lambda i: (i % n_s, 0))
    ins = [x3, w["gmix"], w["wtok"], w["wavt"], w["wqb"], w["wkb"], w["wbvt"],
           consts["ga"], consts["gb"], w["gaq"], w["gak"], w["gqa"], w["gkva"], w["gbq"],
           w["gbk"], w["gkr"], consts["cosa"], consts["sina"], consts["cosb"], consts["sinb"]]
    in_specs = [row(D_MODEL)] + [_resident(a.shape) for a in ins[1:16]] + [tab] * 4
    out_shape = [
        jax.ShapeDtypeStruct((BATCH, SEQ, 1024), BF16),
        jax.ShapeDtypeStruct((BATCH, SEQ, 512), BF16),
        jax.ShapeDtypeStruct((BATCH, SEQ // TK, 512, TK), BF16),
        jax.ShapeDtypeStruct((BATCH, SEQ, 1024), BF16),
        jax.ShapeDtypeStruct((BATCH, SEQ, 1024), BF16),
        jax.ShapeDtypeStruct((BATCH, SEQ // TK, 512, TK), BF16),
    ]
    out_specs = [row(1024), row(512), vt(512), row(1024), row(1024), vt(512)]
    return pl.pallas_call(_proj_even_kernel, grid=grid, in_specs=in_specs, out_specs=out_specs,
                          out_shape=out_shape, compiler_params=_params(1),
                          name="proj_even")(*ins)


def _proj_odd_kernel(x_ref, gmix_ref, wqk_ref, wvt_ref, q_ref, k_ref, vt_ref):
    tm = x_ref.shape[0]
    hb = _rms(x_ref[...], gmix_ref[...]).astype(BF16)
    qk = jnp.dot(hb, wqk_ref[...], preferred_element_type=F32)
    q_ref[...] = _split_halves(qk[:, :C_WIDTH] * (C_HEAD_DIM ** -0.5 * LOG2E)).astype(BF16)
    k_ref[...] = qk[:, C_WIDTH:].astype(BF16)
    vt = lax.dot_general(wvt_ref[...], hb, NT_DIMS, preferred_element_type=F32)
    for c in range(tm // TK):
        vt_ref[c] = vt[:, c * TK:(c + 1) * TK].astype(BF16)


def _proj_odd(x3, gmix, wqk, wvt):
    n_s = SEQ // TM
    row = lambda n: pl.BlockSpec((None, TM, n), lambda i: (i // n_s, i % n_s, 0))
    vt = pl.BlockSpec((None, TM // TK, C_WIDTH, TK), lambda i: (i // n_s, i % n_s, 0, 0))
    out_shape = [
        jax.ShapeDtypeStruct((BATCH, SEQ, 2 * C_WIDTH), BF16),
        jax.ShapeDtypeStruct((BATCH, SEQ, C_WIDTH), BF16),
        jax.ShapeDtypeStruct((BATCH, SEQ // TK, C_WIDTH, TK), BF16),
    ]
    return pl.pallas_call(
        _proj_odd_kernel, grid=(BATCH * n_s,),
        in_specs=[row(D_MODEL), _resident(gmix.shape), _resident(wqk.shape), _resident(wvt.shape)],
        out_specs=[row(2 * C_WIDTH), row(C_WIDTH), vt], out_shape=out_shape,
        compiler_params=_params(1), name="proj_odd")(x3, gmix, wqk, wvt)


def _diag_masks(fn):
    r = lax.broadcasted_iota(jnp.int32, (TK, TQ), 0)
    c = lax.broadcasted_iota(jnp.int32, (TK, TQ), 1)
    return fn(r, c), fn(r + TK, c)


A_ACC = A_V_DIM + 16
B_ACC = B_V_DIM + 16
SUM_ROWS = 16


def _softmax_rest(s, vt, mask, tmax, c, ql, m_ref, acc_ref, rows):
    if mask is not None:
        s = jnp.where(mask, s, -jnp.inf)
    if tmax is None:
        tmax = jnp.max(s, axis=0, keepdims=True)
    row = slice(8 * c, 8 * c + 1)
    m = m_ref[row, ql]
    m_new = jnp.maximum(m, tmax)
    alpha = jnp.exp2(m - m_new)
    p = jnp.exp2(s - m_new)
    m_ref[row, ql] = m_new
    ones_row = lax.broadcasted_iota(jnp.int32, (SUM_ROWS, vt.shape[1]), 0) == 0
    vt_aug = jnp.concatenate([vt, jnp.where(ones_row, 1.0, 0.0).astype(BF16)], axis=0)
    acc_ref[rows, ql] = (alpha * acc_ref[rows, ql]
                         + jnp.dot(vt_aug, p.astype(BF16), preferred_element_type=F32))


ALL_Q = slice(0, TQ)
EARLY_Q = slice(0, TQ // 2)
LATE_Q = slice(TQ // 2, TQ)


def _softmax_attention(i, n, scores, rest, s_ref, t_ref, m_ref, acc_ref):
    acc_ref[...] = jnp.zeros_like(acc_ref)
    m_ref[...] = jnp.full_like(m_ref, NEG_BIG)

    def phase(j, slot, ql, mask, next_ql):
        for c in range(n):
            if next_ql is not None:
                nxt = scores(j + 1, c, next_ql)
                s_ref[1 - slot, c, :, next_ql] = nxt
                if next_ql == ALL_Q:
                    t_ref[1 - slot, 8 * c:8 * c + 1, :] = jnp.max(nxt, axis=0, keepdims=True)
            tmax = None if mask is not None else t_ref[slot, 8 * c:8 * c + 1, :]
            rest(j, c, ql, s_ref[slot, c, :, ql], None if mask is None else mask[:, ql], tmax)

    for c in range(n):
        first = scores(0, c, ALL_Q)
        s_ref[0, c] = first
        t_ref[0, 8 * c:8 * c + 1, :] = jnp.max(first, axis=0, keepdims=True)

    def body(u, carry):
        phase(2 * u, 0, ALL_Q, None, ALL_Q)
        phase(2 * u + 1, 1, ALL_Q, None, ALL_Q)
        return carry

    lax.fori_loop(0, i, body, 0)
    mask0, mask1 = _diag_masks(lambda kp, qp: (kp >> 6) <= (qp >> 6))
    phase(2 * i, 0, ALL_Q, mask0, LATE_Q)
    phase(2 * i + 1, 1, LATE_Q, mask1, None)


def _attn_a_kernel(q_ref, k_ref, vt_ref, lam_ref, gout_ref, o_ref, s_ref, t_ref, acc_ref, m_ref,
                   *, lam0):
    i = pl.program_id(1)

    def scores(j, c, ql):
        h = c // 2
        k = k_ref[pl.ds(pl.multiple_of(j * TK, TK), TK), LANES * h:LANES * (h + 1)]
        return lax.dot_general(k, q_ref[ql, LANES * c:LANES * (c + 1)], NT_DIMS,
                               preferred_element_type=F32)

    def rest(j, c, ql, s, mask, tmax):
        h = c // 2
        _softmax_rest(s, vt_ref[j, LANES * h:LANES * (h + 1), :], mask, tmax, c, ql, m_ref,
                      acc_ref, slice(A_ACC * c, A_ACC * (c + 1)))

    _softmax_attention(i, 2 * A_HEADS, scores, rest, s_ref, t_ref, m_ref, acc_ref)
    al = lam_ref[...]
    lam = (jnp.exp(jnp.sum(al[0:1] * al[1:2], axis=1, keepdims=True))
           - jnp.exp(jnp.sum(al[2:3] * al[3:4], axis=1, keepdims=True)) + lam0)
    for h in range(A_HEADS):
        outs = []
        for c in (2 * h, 2 * h + 1):
            num = acc_ref[A_ACC * c:A_ACC * c + A_V_DIM]
            den = acc_ref[A_ACC * c + A_V_DIM:A_ACC * c + A_V_DIM + 1]
            outs.append(num * (1.0 / den))
        o1, o2 = outs
        o = o1 - lam * o2
        o = o * lax.rsqrt(jnp.mean(o * o, axis=0, keepdims=True) + NORM_EPS)
        o_ref[:, LANES * h:LANES * (h + 1)] = (o.T * gout_ref[...]).astype(BF16)


def _kv_spec(n_lanes):
    return pl.BlockSpec((None, SEQ, n_lanes), lambda b, i: (b, 0, 0))


def _vt_spec(n_rows):
    return pl.BlockSpec((None, SEQ // TK, n_rows, TK), lambda b, i: (b, 0, 0, 0))


def _attn_a(aq, ak, avt, a_lambda, gout, lam0):
    n_q = SEQ // TQ
    n_chain = 2 * A_HEADS
    kern = functools.partial(_attn_a_kernel, lam0=lam0)
    return pl.pallas_call(
        kern, grid=(BATCH, n_q),
        in_specs=[pl.BlockSpec((None, TQ, n_chain * LANES), lambda b, i: (b, i, 0)),
                  _kv_spec(512), _vt_spec(512),
                  _resident(a_lambda.shape), _resident(gout.shape)],
        out_specs=pl.BlockSpec((None, TQ, 512), lambda b, i: (b, i, 0)),
        out_shape=jax.ShapeDtypeStruct((BATCH, SEQ, 512), BF16),
        scratch_shapes=[pltpu.VMEM((2, n_chain, TK, TQ), F32), pltpu.VMEM((2, 8 * n_chain, TQ), F32),
                        pltpu.VMEM((n_chain * A_ACC, TQ), F32), pltpu.VMEM((8 * n_chain, TQ), F32)],
        compiler_params=_params(2), name="attn_a")(aq, ak, avt, a_lambda, gout)


def _attn_b_kernel(q_ref, k_ref, vt_ref, o_ref, s_ref, t_ref, acc_ref, m_ref):
    i = pl.program_id(1)

    def scores(j, h, ql):
        cols = slice(LANES * h, LANES * (h + 1))
        return lax.dot_general(k_ref[pl.ds(pl.multiple_of(j * TK, TK), TK), cols], q_ref[ql, cols],
                               NT_DIMS, preferred_element_type=F32)

    def rest(j, h, ql, s, mask, tmax):
        _softmax_rest(s, vt_ref[j, B_V_DIM * h:B_V_DIM * (h + 1), :], mask, tmax, h, ql, m_ref,
                      acc_ref, slice(B_ACC * h, B_ACC * (h + 1)))

    _softmax_attention(i, B_HEADS, scores, rest, s_ref, t_ref, m_ref, acc_ref)
    for hp in range(B_HEADS // 2):
        parts = []
        for h in (2 * hp, 2 * hp + 1):
            num = acc_ref[B_ACC * h:B_ACC * h + B_V_DIM]
            den = acc_ref[B_ACC * h + B_V_DIM:B_ACC * h + B_V_DIM + 1]
            parts.append(num * (1.0 / den))
        o_ref[:, LANES * hp:LANES * (hp + 1)] = jnp.concatenate(parts, axis=0).T.astype(BF16)


def _attn_b(bq, bk, bvt):
    n_q = SEQ // TQ
    return pl.pallas_call(
        _attn_b_kernel, grid=(BATCH, n_q),
        in_specs=[pl.BlockSpec((None, TQ, 1024), lambda b, i: (b, i, 0)),
                  _kv_spec(1024), _vt_spec(512)],
        out_specs=pl.BlockSpec((None, TQ, 512), lambda b, i: (b, i, 0)),
        out_shape=jax.ShapeDtypeStruct((BATCH, SEQ, 512), BF16),
        scratch_shapes=[pltpu.VMEM((2, B_HEADS, TK, TQ), F32), pltpu.VMEM((2, 8 * B_HEADS, TQ), F32),
                        pltpu.VMEM((B_HEADS * B_ACC, TQ), F32), pltpu.VMEM((8 * B_HEADS, TQ), F32)],
        compiler_params=_params(2), name="attn_b")(bq, bk, bvt)


def _stick_front(z, u_ref, mask):
    log1p2 = jnp.log(1.0 + jnp.exp2(jnp.minimum(z, 126.0))) * LOG2E
    sp = jnp.maximum(z, log1p2)
    logsig = z - sp
    if mask is not None:
        sp = jnp.where(mask, sp, 0.0)
    tl = jnp.dot(u_ref[...], sp.astype(BF16), preferred_element_type=F32)
    return logsig, tl[0:1, :] + sp[0:1, :], tl


def _stick_back(front, vt, mask, c, tail_ref, acc_ref, rows, ql):
    logsig, tile_mass, tl = front
    row = slice(8 * c, 8 * c + 1)
    tail = tail_ref[row, ql]
    w = jnp.exp2(logsig - tl)
    if mask is not None:
        w = jnp.where(mask, w, 0.0)
    pv = jnp.dot(vt, w.astype(BF16), preferred_element_type=F32)
    acc_ref[rows, ql] = acc_ref[rows, ql] + pv * jnp.exp2(-tail)
    tail_ref[row, ql] = tail + tile_mass


def _attn_c_kernel(q_ref, k_ref, vt_ref, u_ref, o_ref, s_ref, acc_ref, tail_ref):
    i = pl.program_id(1)
    n = C_HEADS

    def scores(j, h, ql):
        k = k_ref[pl.ds(pl.multiple_of(j * TK, TK), TK), LANES * (h // 2):LANES * (h // 2 + 1)]
        return lax.dot_general(k, q_ref[ql, LANES * h:LANES * (h + 1)], NT_DIMS,
                               preferred_element_type=F32)

    def phase(j, slot, ql, mask, j_next, next_ql=ALL_Q):
        fronts = {}
        qmask = None if mask is None else mask[:, ql]
        for c in range(n + STICK_LAG):
            if c < n:
                s_ref[1 - slot, c, :, next_ql] = scores(j_next, c, next_ql)
                fronts[c] = _stick_front(s_ref[slot, c, :, ql], u_ref, qmask)
            d = c - STICK_LAG
            if d >= 0:
                rows = slice(C_HEAD_DIM * d, C_HEAD_DIM * (d + 1))
                _stick_back(fronts.pop(d), vt_ref[j, rows, :], qmask, d, tail_ref, acc_ref,
                            rows, ql)

    acc_ref[...] = jnp.zeros_like(acc_ref)
    state_row = lax.broadcasted_iota(jnp.int32, tail_ref.shape, 0)
    tail_ref[...] = jnp.where((state_row & 7) == 0, 0.0, -NEG_BIG)
    for c in range(n):
        s_ref[0, c, :, LATE_Q] = scores(2 * i + 1, c, LATE_Q)
    mask0, mask1 = _diag_masks(lambda kp, qp: kp < qp)
    phase(2 * i + 1, 0, LATE_Q, mask1, 2 * i)
    phase(2 * i, 1, ALL_Q, mask0, jnp.maximum(2 * i - 1, 0))

    def min_tail(ql=ALL_Q):
        return jnp.min(tail_ref[:, ql])

    def live_phase(j, slot, j_next):
        late_alive = min_tail(LATE_Q) < DEAD_TAIL

        @pl.when(late_alive)
        def _():
            phase(j, slot, ALL_Q, None, j_next, ALL_Q)

        @pl.when(jnp.logical_not(late_alive))
        def _():
            phase(j, slot, EARLY_Q, None, j_next, EARLY_Q)

    def cond(carry):
        u, tmin = carry
        return jnp.logical_and(u < i, tmin < DEAD_TAIL)

    def body(carry):
        u, _ = carry
        j = 2 * i - 1 - 2 * u
        live_phase(j, 0, j - 1)

        @pl.when(min_tail() < DEAD_TAIL)
        def _():
            live_phase(j - 1, 1, jnp.maximum(j - 2, 0))

        return u + 1, min_tail()

    lax.while_loop(cond, body, (jnp.int32(0), min_tail()))
    for hp in range(C_HEADS // 2):
        blk = slice(LANES * hp, LANES * (hp + 1))
        o_ref[:, blk] = acc_ref[blk].T.astype(BF16)


def _attn_c(q, k, vt, u):
    n_q = SEQ // TQ
    return pl.pallas_call(
        _attn_c_kernel, grid=(BATCH, n_q),
        in_specs=[pl.BlockSpec((None, TQ, 2 * C_WIDTH), lambda b, i: (b, i, 0)),
                  _kv_spec(C_WIDTH), _vt_spec(C_WIDTH), _resident(u.shape)],
        out_specs=pl.BlockSpec((None, TQ, C_WIDTH), lambda b, i: (b, i, 0)),
        out_shape=jax.ShapeDtypeStruct((BATCH, SEQ, C_WIDTH), BF16),
        scratch_shapes=[pltpu.VMEM((2, C_HEADS, TK, TQ), F32), pltpu.VMEM((C_WIDTH, TQ), F32),
                        pltpu.VMEM((8 * C_HEADS, TQ), F32)],
        compiler_params=_params(2), name="attn_c")(q, k, vt, u)


def _ffn_kernel(*refs):
    x_ref, part_refs = refs[0], refs[1:-7]
    wo_ref, g_ref, wg_ref, wu_ref, wd_ref, o_ref, a_ref = refs[-7:]
    x = x_ref[...]
    off = 0
    for p_ref in part_refs:
        n = p_ref.shape[1]
        x = x + jnp.dot(p_ref[...], wo_ref[off:off + n, :], preferred_element_type=F32)
        off += n
    hb = _rms(x, g_ref[...]).astype(BF16)
    for c in range(D_FF // FF_CHUNK):
        sl = slice(c * FF_CHUNK, (c + 1) * FF_CHUNK)
        g = jnp.dot(hb, wg_ref[:, sl], preferred_element_type=F32)
        u = jnp.dot(hb, wu_ref[:, sl], preferred_element_type=F32)
        a_ref[:, sl] = (g * (1.0 / (1.0 + jnp.exp(-g))) * u).astype(BF16)
    o_ref[...] = x + jnp.dot(a_ref[...], wd_ref[...], preferred_element_type=F32)


def _outproj_ffn(x2, parts, w_out, gain, wg, wu, wd):
    t = x2.shape[0]
    row = lambda n: pl.BlockSpec((TM_FFN, n), lambda i: (i, 0))
    weights = [w_out, gain, wg, wu, wd]
    return pl.pallas_call(
        _ffn_kernel, grid=(t // TM_FFN,),
        in_specs=([row(D_MODEL)] + [row(p.shape[1]) for p in parts]
                  + [_resident(w.shape) for w in weights]),
        out_specs=row(D_MODEL), out_shape=jax.ShapeDtypeStruct(x2.shape, F32),
        scratch_shapes=[pltpu.VMEM((TM_FFN, D_FF), BF16)],
        compiler_params=_params(1), name="outproj_ffn")(x2, *parts, *weights)


def _consts():
    lane = np.arange(LANES)
    pos = jnp.arange(SEQ, dtype=F32)[:, None]

    def table(half, idx, active, neg):
        inv_freq = jnp.power(ROPE_THETA, -jnp.arange(half, dtype=F32) / half)
        ang = pos * inv_freq[idx][None, :]
        act = jnp.asarray(active)[None, :]
        sign = jnp.asarray(np.where(neg, -1.0, 1.0), F32)[None, :]
        return jnp.where(act, jnp.cos(ang), 1.0), jnp.where(act, jnp.sin(ang) * sign, 0.0)

    cosa, sina = table(32, lane % 32, np.ones(LANES, bool), (lane % 64) < 32)
    rope_b = (lane >= 64) & (lane < 96)
    cosb, sinb = table(16, (lane - 64) % 16, rope_b, lane < 80)

    i256 = np.arange(MXU_DIM)
    ga = ((i256[:, None] // 64) == (i256[None, :] // 64)) / 64.0
    blk = i256 // LANES
    w = i256 % LANES
    grp = np.where(w < 64, 0, np.where(w < 96, 1, 2))
    same = (blk[:, None] == blk[None, :]) & (grp[:, None] == grp[None, :])
    scale = np.where(grp == 0, 1.0 / 64, np.where(grp == 1, 1.0 / 32, 0.0))
    gb = same * scale[None, :]
    it = np.arange(TK)
    u = (it[None, :] > it[:, None]).astype(np.float32)
    return dict(cosa=cosa, sina=sina, cosb=cosb, sinb=sinb,
                ga=jnp.asarray(ga, BF16), gb=jnp.asarray(gb, BF16), u=jnp.asarray(u, BF16))


def _pad_heads(wm, n_heads, width):
    k = wm.shape[0]
    w3 = wm.reshape(k, n_heads, width)
    return jnp.pad(w3, ((0, 0), (0, 0), (0, LANES - width))).reshape(k, n_heads * LANES)


def _tile_gain(parts, reps, scale=1.0):
    g = jnp.concatenate([p.astype(F32) for p in parts])
    g = jnp.pad(g, (0, LANES - g.shape[0]))
    return (jnp.tile(g, reps) * scale)[None, :]


def _even_weights(i, l, norm_mix, ab_w_in, a_q_norm, a_k_norm, a_out_norm, b_q_a_norm, b_w_q_b,
                  b_kv_a_norm, b_w_kv_b, b_q_nope_norm, b_q_rope_norm, b_k_nope_norm,
                  b_k_rope_norm):
    w_in = ab_w_in[i]
    kr_cols = jnp.pad(w_in[:, 1920:1952], ((0, 0), (64, 32)))
    wtok = jnp.concatenate([w_in[:, 0:1024], w_in[:, 1536:1920], kr_cols], axis=1).astype(BF16)
    wkv = b_w_kv_b[i].reshape(B_KV_RANK, B_HEADS, B_NOPE_DIM + B_V_DIM)
    sa = (A_QK_DIM ** -0.5) * LOG2E
    sb = ((B_NOPE_DIM + B_ROPE_DIM) ** -0.5) * LOG2E
    zeros32 = jnp.zeros((32,), F32)
    return dict(
        gmix=norm_mix[l][None, :],
        wtok=wtok,
        wavt=w_in[:, 1024:1536].T.astype(BF16),
        wqb=_pad_heads(b_w_q_b[i], B_HEADS, B_NOPE_DIM + B_ROPE_DIM).astype(BF16),
        wkb=_pad_heads(wkv[:, :, :B_NOPE_DIM].reshape(B_KV_RANK, -1), B_HEADS,
                       B_NOPE_DIM).astype(BF16),
        wbvt=wkv[:, :, B_NOPE_DIM:].reshape(B_KV_RANK, -1).T.astype(BF16),
        gaq=_tile_gain([a_q_norm[i], a_q_norm[i]], 4, sa),
        gak=_tile_gain([a_k_norm[i], a_k_norm[i]], 4),
        gqa=b_q_a_norm[i][None, :],
        gkva=b_kv_a_norm[i][None, :],
        gbq=_tile_gain([b_q_nope_norm[i], b_q_rope_norm[i]], 8, sb),
        gbk=_tile_gain([b_k_nope_norm[i]], 8),
        gkr=_tile_gain([jnp.zeros((64,), F32), b_k_rope_norm[i], zeros32], 1),
        gout=a_out_norm[i][None, :] * (1.0 - _lambda_init(l)),
    )


def kernel(x, norm_mix, norm_ffn, ab_w_in, a_q_norm, a_k_norm, a_lambda, a_out_norm, b_q_a_norm,
           b_w_q_b, b_kv_a_norm, b_w_kv_b, b_q_nope_norm, b_q_rope_norm, b_k_nope_norm,
           b_k_rope_norm, ab_w_out, c_w_in, c_w_out, ffn_w_gate, ffn_w_up, ffn_w_down):
    consts = _consts()
    t = BATCH * SEQ
    for l in range(DEPTH):
        i = l // 2
        x3 = x.reshape(BATCH, SEQ, D_MODEL)
        if l % 2 == 0:
            w = _even_weights(i, l, norm_mix, ab_w_in, a_q_norm, a_k_norm, a_out_norm, b_q_a_norm,
                              b_w_q_b, b_kv_a_norm, b_w_kv_b, b_q_nope_norm, b_q_rope_norm,
                              b_k_nope_norm, b_k_rope_norm)
            aq, ak, avt, bq, bk, bvt = _proj_even(x3, consts, w)
            out_a = _attn_a(aq, ak, avt, a_lambda[i], w["gout"], _lambda_init(l))
            out_b = _attn_b(bq, bk, bvt)
            parts = [out_a.reshape(t, 512), out_b.reshape(t, 512)]
            w_out = ab_w_out[i].astype(BF16)
        else:
            w_in = c_w_in[i]
            q, k, vt = _proj_odd(x3, norm_mix[l][None, :], w_in[:, :2 * C_WIDTH].astype(BF16),
                                 w_in[:, 2 * C_WIDTH:].T.astype(BF16))
            out_c = _attn_c(q, k, vt, consts["u"])
            parts = [out_c.reshape(t, C_WIDTH)]
            w_out = c_w_out[i].astype(BF16)
        x2 = _outproj_ffn(x.reshape(t, D_MODEL), parts, w_out, norm_ffn[l][None, :],
                          ffn_w_gate[l].astype(BF16), ffn_w_up[l].astype(BF16),
                          ffn_w_down[l].astype(BF16))
        x = x2.reshape(BATCH, SEQ, D_MODEL)
    return x
```

```python
import functools
import math

import jax
import jax.numpy as jnp
import numpy as np
from jax import lax
from jax.experimental import pallas as pl
from jax.experimental.pallas import tpu as pltpu

D_MODEL = 1024
BATCH = 16
SEQ = 2048
DEPTH = 4
CHUNK = 64
ROPE_THETA = 10000.0
NORM_EPS = 1e-6

A_HEADS = 4
A_QK_DIM = 64
A_V_DIM = 128
B_HEADS = 8
B_Q_RANK = 256
B_KV_RANK = 128
B_NOPE_DIM = 64
B_ROPE_DIM = 32
B_V_DIM = 64
C_HEADS = 16
C_HEAD_DIM = 64
C_WIDTH = C_HEADS * C_HEAD_DIM
D_FF = 2816

LANES = 128
MXU_DIM = 256
TQ = 512
TK = 256
TM = 512
TM_FFN = 1024
FF_CHUNK = 256
VMEM_LIMIT_BYTES = 56 * 1024 * 1024
LOG2E = 1.4426950408889634
NEG_BIG = -1e30
DEAD_TAIL = 151.0
STICK_LAG = 1

F32 = jnp.float32
BF16 = jnp.bfloat16
NT_DIMS = (((1,), (1,)), ((), ()))


def _lambda_init(layer_idx):
    return 0.8 - 0.6 * math.exp(-0.3 * layer_idx)


def _params(n_axes):
    return pltpu.CompilerParams(dimension_semantics=("arbitrary",) * n_axes,
                                vmem_limit_bytes=VMEM_LIMIT_BYTES)


def _resident(shape):
    zeros = (0,) * len(shape)
    return pl.BlockSpec(shape, lambda *_: zeros, pipeline_mode=pl.Buffered(1))


def _rms(x, gain):
    ms = jnp.mean(x * x, axis=1, keepdims=True)
    return x * lax.rsqrt(ms + NORM_EPS) * gain


def _group_mean_sq(y, g_ref):
    outs = []
    for c in range(y.shape[1] // MXU_DIM):
        yc = y[:, c * MXU_DIM:(c + 1) * MXU_DIM]
        outs.append(jnp.dot((yc * yc).astype(BF16), g_ref[...], preferred_element_type=F32))
    return jnp.concatenate(outs, axis=1)


def _group_rms(y, g_ref, gain):
    return y * lax.rsqrt(_group_mean_sq(y, g_ref) + NORM_EPS) * gain


def _rope(y, first_half, half, cos, sin_signed):
    n = y.shape[1]
    partner = jnp.where(first_half, pltpu.roll(y, n - half, 1), pltpu.roll(y, half, 1))
    return y * cos + partner * sin_signed


def _split_halves(q):
    lo_lanes = lax.broadcasted_iota(jnp.int32, (q.shape[0], LANES), 1) < 64
    blocks = []
    for b in range(q.shape[1] // LANES):
        blk = q[:, b * LANES:(b + 1) * LANES]
        blocks += [jnp.where(lo_lanes, blk, 0.0), jnp.where(lo_lanes, 0.0, blk)]
    return jnp.concatenate(blocks, axis=1)


def _proj_even_kernel(x_ref, gmix_ref, wtok_ref, wavt_ref, wqb_ref, wkb_ref, wbvt_ref,
                      ga_ref, gb_ref, gaq_ref, gak_ref, gqa_ref, gkva_ref, gbq_ref, gbk_ref,
                      gkr_ref, cosa_ref, sina_ref, cosb_ref, sinb_ref,
                      aq_ref, ak_ref, avt_ref, bq_ref, bk_ref, bvt_ref):
    tm = x_ref.shape[0]
    hb = _rms(x_ref[...], gmix_ref[...]).astype(BF16)
    p = jnp.dot(hb, wtok_ref[...], preferred_element_type=F32)

    lane_a = lax.broadcasted_iota(jnp.int32, (tm, 512), 1)
    first_a = (lane_a & 32) == 0
    cosa = jnp.concatenate([cosa_ref[...]] * 4, axis=1)
    sina = jnp.concatenate([sina_ref[...]] * 4, axis=1)
    aq = _rope(_group_rms(p[:, 0:512], ga_ref, gaq_ref[...]), first_a, 32, cosa, sina)
    aq_ref[...] = _split_halves(aq).astype(BF16)
    ak = _rope(_group_rms(p[:, 512:1024], ga_ref, gak_ref[...]), first_a, 32, cosa, sina)
    ak_ref[...] = ak.astype(BF16)
    avt = lax.dot_general(wavt_ref[...], hb, NT_DIMS, preferred_element_type=F32)
    for c in range(tm // TK):
        avt_ref[c] = avt[:, c * TK:(c + 1) * TK].astype(BF16)

    qln = _rms(p[:, 1024:1280], gqa_ref[...]).astype(BF16)
    qb = jnp.dot(qln, wqb_ref[...], preferred_element_type=F32)
    lane_b = lax.broadcasted_iota(jnp.int32, (tm, 1024), 1)
    first_b = (lane_b & 127) < 80
    cosb = jnp.concatenate([cosb_ref[...]] * 8, axis=1)
    sinb = jnp.concatenate([sinb_ref[...]] * 8, axis=1)
    bq = _rope(_group_rms(qb, gb_ref, gbq_ref[...]), first_b, 16, cosb, sinb)
    bq_ref[...] = bq.astype(BF16)

    kvn = _rms(p[:, 1280:1408], gkva_ref[...]).astype(BF16)
    kb = _group_rms(jnp.dot(kvn, wkb_ref[...], preferred_element_type=F32),
                    gb_ref, gbk_ref[...])
    kr = p[:, 1408:1536]
    krn = kr * lax.rsqrt(jnp.sum(kr * kr, axis=1, keepdims=True) * (1.0 / B_ROPE_DIM)
                         + NORM_EPS) * gkr_ref[...]
    lane_r = lax.broadcasted_iota(jnp.int32, (tm, 128), 1)
    krr = _rope(krn, lane_r < 80, 16, cosb_ref[...], sinb_ref[...])
    bk_ref[...] = (kb + jnp.concatenate([krr] * 8, axis=1)).astype(BF16)
    bvt = lax.dot_general(wbvt_ref[...], kvn, NT_DIMS, preferred_element_type=F32)
    for c in range(tm // TK):
        bvt_ref[c] = bvt[:, c * TK:(c + 1) * TK].astype(BF16)


def _proj_even(x3, consts, w):
    n_s = SEQ // TM
    grid = (BATCH * n_s,)
    row = lambda n: pl.BlockSpec((None, TM, n), lambda i: (i // n_s, i % n_s, 0))
    vt = lambda n: pl.BlockSpec((None, TM // TK, n, TK), lambda i: (i // n_s, i % n_s, 0, 0))
    tab = pl.BlockSpec((TM, LANES), lambda i: (i % n_s, 0))
    ins = [x3, w["gmix"], w["wtok"], w["wavt"], w["wqb"], w["wkb"], w["wbvt"],
           consts["ga"], consts["gb"], w["gaq"], w["gak"], w["gqa"], w["gkva"], w["gbq"],
           w["gbk"], w["gkr"], consts["cosa"], consts["sina"], consts["cosb"], consts["sinb"]]
    in_specs = [row(D_MODEL)] + [_resident(a.shape) for a in ins[1:16]] + [tab] * 4
    out_shape = [
        jax.ShapeDtypeStruct((BATCH, SEQ, 1024), BF16),
        jax.ShapeDtypeStruct((BATCH, SEQ, 512), BF16),
        jax.ShapeDtypeStruct((BATCH, SEQ // TK, 512, TK), BF16),
        jax.ShapeDtypeStruct((BATCH, SEQ, 1024), BF16),
        jax.ShapeDtypeStruct((BATCH, SEQ, 1024), BF16),
        jax.ShapeDtypeStruct((BATCH, SEQ // TK, 512, TK), BF16),
    ]
    out_specs = [row(1024), row(512), vt(512), row(1024), row(1024), vt(512)]
    return pl.pallas_call(_proj_even_kernel, grid=grid, in_specs=in_specs, out_specs=out_specs,
                          out_shape=out_shape, compiler_params=_params(1),
                          name="proj_even")(*ins)


def _proj_odd_kernel(x_ref, gmix_ref, wqk_ref, wvt_ref, q_ref, k_ref, vt_ref):
    tm = x_ref.shape[0]
    hb = _rms(x_ref[...], gmix_ref[...]).astype(BF16)
    qk = jnp.dot(hb, wqk_ref[...], preferred_element_type=F32)
    q_ref[...] = _split_halves(qk[:, :C_WIDTH] * (C_HEAD_DIM ** -0.5 * LOG2E)).astype(BF16)
    k_ref[...] = qk[:, C_WIDTH:].astype(BF16)
    vt = lax.dot_general(wvt_ref[...], hb, NT_DIMS, preferred_element_type=F32)
    for c in range(tm // TK):
        vt_ref[c] = vt[:, c * TK:(c + 1) * TK].astype(BF16)


def _proj_odd(x3, gmix, wqk, wvt):
    n_s = SEQ // TM
    row = lambda n: pl.BlockSpec((None, TM, n), lambda i: (i // n_s, i % n_s, 0))
    vt = pl.BlockSpec((None, TM // TK, C_WIDTH, TK), lambda i: (i // n_s, i % n_s, 0, 0))
    out_shape = [
        jax.ShapeDtypeStruct((BATCH, SEQ, 2 * C_WIDTH), BF16),
        jax.ShapeDtypeStruct((BATCH, SEQ, C_WIDTH), BF16),
        jax.ShapeDtypeStruct((BATCH, SEQ // TK, C_WIDTH, TK), BF16),
    ]
    return pl.pallas_call(
        _proj_odd_kernel, grid=(BATCH * n_s,),
        in_specs=[row(D_MODEL), _resident(gmix.shape), _resident(wqk.shape), _resident(wvt.shape)],
        out_specs=[row(2 * C_WIDTH), row(C_WIDTH), vt], out_shape=out_shape,
        compiler_params=_params(1), name="proj_odd")(x3, gmix, wqk, wvt)


def _diag_masks(fn):
    r = lax.broadcasted_iota(jnp.int32, (TK, TQ), 0)
    c = lax.broadcasted_iota(jnp.int32, (TK, TQ), 1)
    return fn(r, c), fn(r + TK, c)


A_ACC = A_V_DIM + 16
B_ACC = B_V_DIM + 16
SUM_ROWS = 16


def _softmax_rest(s, vt, mask, tmax, c, ql, m_ref, acc_ref, rows):
    if mask is not None:
        s = jnp.where(mask, s, -jnp.inf)
    if tmax is None:
        tmax = jnp.max(s, axis=0, keepdims=True)
    row = slice(8 * c, 8 * c + 1)
    m = m_ref[row, ql]
    m_new = jnp.maximum(m, tmax)
    alpha = jnp.exp2(m - m_new)
    p = jnp.exp2(s - m_new)
    m_ref[row, ql] = m_new
    ones_row = lax.broadcasted_iota(jnp.int32, (SUM_ROWS, vt.shape[1]), 0) == 0
    vt_aug = jnp.concatenate([vt, jnp.where(ones_row, 1.0, 0.0).astype(BF16)], axis=0)
    acc_ref[rows, ql] = (alpha * acc_ref[rows, ql]
                         + jnp.dot(vt_aug, p.astype(BF16), preferred_element_type=F32))


ALL_Q = slice(0, TQ)
EARLY_Q = slice(0, TQ // 2)
LATE_Q = slice(TQ // 2, TQ)


def _softmax_attention(i, n, scores, rest, s_ref, t_ref, m_ref, acc_ref):
    acc_ref[...] = jnp.zeros_like(acc_ref)
    m_ref[...] = jnp.full_like(m_ref, NEG_BIG)

    def phase(j, slot, ql, mask, next_ql):
        for c in range(n):
            if next_ql is not None:
                nxt = scores(j + 1, c, next_ql)
                s_ref[1 - slot, c, :, next_ql] = nxt
                if next_ql == ALL_Q:
                    t_ref[1 - slot, 8 * c:8 * c + 1, :] = jnp.max(nxt, axis=0, keepdims=True)
            tmax = None if mask is not None else t_ref[slot, 8 * c:8 * c + 1, :]
            rest(j, c, ql, s_ref[slot, c, :, ql], None if mask is None else mask[:, ql], tmax)

    for c in range(n):
        first = scores(0, c, ALL_Q)
        s_ref[0, c] = first
        t_ref[0, 8 * c:8 * c + 1, :] = jnp.max(first, axis=0, keepdims=True)

    def body(u, carry):
        phase(2 * u, 0, ALL_Q, None, ALL_Q)
        phase(2 * u + 1, 1, ALL_Q, None, ALL_Q)
        return carry

    lax.fori_loop(0, i, body, 0)
    mask0, mask1 = _diag_masks(lambda kp, qp: (kp >> 6) <= (qp >> 6))
    phase(2 * i, 0, ALL_Q, mask0, LATE_Q)
    phase(2 * i + 1, 1, LATE_Q, mask1, None)


def _attn_a_kernel(q_ref, k_ref, vt_ref, lam_ref, gout_ref, o_ref, s_ref, t_ref, acc_ref, m_ref,
                   *, lam0):
    i = pl.program_id(1)

    def scores(j, c, ql):
        h = c // 2
        k = k_ref[pl.ds(pl.multiple_of(j * TK, TK), TK), LANES * h:LANES * (h + 1)]
        return lax.dot_general(k, q_ref[ql, LANES * c:LANES * (c + 1)], NT_DIMS,
                               preferred_element_type=F32)

    def rest(j, c, ql, s, mask, tmax):
        h = c // 2
        _softmax_rest(s, vt_ref[j, LANES * h:LANES * (h + 1), :], mask, tmax, c, ql, m_ref,
                      acc_ref, slice(A_ACC * c, A_ACC * (c + 1)))

    _softmax_attention(i, 2 * A_HEADS, scores, rest, s_ref, t_ref, m_ref, acc_ref)
    al = lam_ref[...]
    lam = (jnp.exp(jnp.sum(al[0:1] * al[1:2], axis=1, keepdims=True))
           - jnp.exp(jnp.sum(al[2:3] * al[3:4], axis=1, keepdims=True)) + lam0)
    for h in range(A_HEADS):
        outs = []
        for c in (2 * h, 2 * h + 1):
            num = acc_ref[A_ACC * c:A_ACC * c + A_V_DIM]
            den = acc_ref[A_ACC * c + A_V_DIM:A_ACC * c + A_V_DIM + 1]
            outs.append(num * (1.0 / den))
        o1, o2 = outs
        o = o1 - lam * o2
        o = o * lax.rsqrt(jnp.mean(o * o, axis=0, keepdims=True) + NORM_EPS)
        o_ref[:, LANES * h:LANES * (h + 1)] = (o.T * gout_ref[...]).astype(BF16)


def _kv_spec(n_lanes):
    return pl.BlockSpec((None, SEQ, n_lanes), lambda b, i: (b, 0, 0))


def _vt_spec(n_rows):
    return pl.BlockSpec((None, SEQ // TK, n_rows, TK), lambda b, i: (b, 0, 0, 0))


def _attn_a(aq, ak, avt, a_lambda, gout, lam0):
    n_q = SEQ // TQ
    n_chain = 2 * A_HEADS
    kern = functools.partial(_attn_a_kernel, lam0=lam0)
    return pl.pallas_call(
        kern, grid=(BATCH, n_q),
        in_specs=[pl.BlockSpec((None, TQ, n_chain * LANES), lambda b, i: (b, i, 0)),
                  _kv_spec(512), _vt_spec(512),
                  _resident(a_lambda.shape), _resident(gout.shape)],
        out_specs=pl.BlockSpec((None, TQ, 512), lambda b, i: (b, i, 0)),
        out_shape=jax.ShapeDtypeStruct((BATCH, SEQ, 512), BF16),
        scratch_shapes=[pltpu.VMEM((2, n_chain, TK, TQ), F32), pltpu.VMEM((2, 8 * n_chain, TQ), F32),
                        pltpu.VMEM((n_chain * A_ACC, TQ), F32), pltpu.VMEM((8 * n_chain, TQ), F32)],
        compiler_params=_params(2), name="attn_a")(aq, ak, avt, a_lambda, gout)


def _attn_b_kernel(q_ref, k_ref, vt_ref, o_ref, s_ref, t_ref, acc_ref, m_ref):
    i = pl.program_id(1)

    def scores(j, h, ql):
        cols = slice(LANES * h, LANES * (h + 1))
        return lax.dot_general(k_ref[pl.ds(pl.multiple_of(j * TK, TK), TK), cols], q_ref[ql, cols],
                               NT_DIMS, preferred_element_type=F32)

    def rest(j, h, ql, s, mask, tmax):
        _softmax_rest(s, vt_ref[j, B_V_DIM * h:B_V_DIM * (h + 1), :], mask, tmax, h, ql, m_ref,
                      acc_ref, slice(B_ACC * h, B_ACC * (h + 1)))

    _softmax_attention(i, B_HEADS, scores, rest, s_ref, t_ref, m_ref, acc_ref)
    for hp in range(B_HEADS // 2):
        parts = []
        for h in (2 * hp, 2 * hp + 1):
            num = acc_ref[B_ACC * h:B_ACC * h + B_V_DIM]
            den = acc_ref[B_ACC * h + B_V_DIM:B_ACC * h + B_V_DIM + 1]
            parts.append(num * (1.0 / den))
        o_ref[:, LANES * hp:LANES * (hp + 1)] = jnp.concatenate(parts, axis=0).T.astype(BF16)


def _attn_b(bq, bk, bvt):
    n_q = SEQ // TQ
    return pl.pallas_call(
        _attn_b_kernel, grid=(BATCH, n_q),
        in_specs=[pl.BlockSpec((None, TQ, 1024), lambda b, i: (b, i, 0)),
                  _kv_spec(1024), _vt_spec(512)],
        out_specs=pl.BlockSpec((None, TQ, 512), lambda b, i: (b, i, 0)),
        out_shape=jax.ShapeDtypeStruct((BATCH, SEQ, 512), BF16),
        scratch_shapes=[pltpu.VMEM((2, B_HEADS, TK, TQ), F32), pltpu.VMEM((2, 8 * B_HEADS, TQ), F32),
                        pltpu.VMEM((B_HEADS * B_ACC, TQ), F32), pltpu.VMEM((8 * B_HEADS, TQ), F32)],
        compiler_params=_params(2), name="attn_b")(bq, bk, bvt)


def _stick_front(z, u_ref, mask):
    log1p2 = jnp.log(1.0 + jnp.exp2(jnp.minimum(z, 126.0))) * LOG2E
    sp = jnp.maximum(z, log1p2)
    logsig = z - sp
    if mask is not None:
        sp = jnp.where(mask, sp, 0.0)
    tl = jnp.dot(u_ref[...], sp.astype(BF16), preferred_element_type=F32)
    return logsig, tl[0:1, :] + sp[0:1, :], tl


def _stick_back(front, vt, mask, c, tail_ref, acc_ref, rows, ql):
    logsig, tile_mass, tl = front
    row = slice(8 * c, 8 * c + 1)
    tail = tail_ref[row, ql]
    w = jnp.exp2(logsig - tl)
    if mask is not None:
        w = jnp.where(mask, w, 0.0)
    pv = jnp.dot(vt, w.astype(BF16), preferred_element_type=F32)
    acc_ref[rows, ql] = acc_ref[rows, ql] + pv * jnp.exp2(-tail)
    tail_ref[row, ql] = tail + tile_mass


def _attn_c_kernel(q_ref, k_ref, vt_ref, u_ref, o_ref, s_ref, acc_ref, tail_ref):
    i = pl.program_id(1)
    n = C_HEADS

    def scores(j, h, ql):
        k = k_ref[pl.ds(pl.multiple_of(j * TK, TK), TK), LANES * (h // 2):LANES * (h // 2 + 1)]
        return lax.dot_general(k, q_ref[ql, LANES * h:LANES * (h + 1)], NT_DIMS,
                               preferred_element_type=F32)

    def phase(j, slot, ql, mask, j_next, next_ql=ALL_Q):
        fronts = {}
        qmask = None if mask is None else mask[:, ql]
        for c in range(n + STICK_LAG):
            if c < n:
                fronts[c] = _stick_front(s_ref[slot, c, :, ql], u_ref, qmask)
                s_ref[1 - slot, c, :, next_ql] = scores(j_next, c, next_ql)
            d = c - STICK_LAG
            if d >= 0:
                rows = slice(C_HEAD_DIM * d, C_HEAD_DIM * (d + 1))
                _stick_back(fronts.pop(d), vt_ref[j, rows, :], qmask, d, tail_ref, acc_ref,
                            rows, ql)

    acc_ref[...] = jnp.zeros_like(acc_ref)
    state_row = lax.broadcasted_iota(jnp.int32, tail_ref.shape, 0)
    tail_ref[...] = jnp.where((state_row & 7) == 0, 0.0, -NEG_BIG)
    for c in range(n):
        s_ref[0, c, :, LATE_Q] = scores(2 * i + 1, c, LATE_Q)
    mask0, mask1 = _diag_masks(lambda kp, qp: kp < qp)
    phase(2 * i + 1, 0, LATE_Q, mask1, 2 * i)
    phase(2 * i, 1, ALL_Q, mask0, jnp.maximum(2 * i - 1, 0))

    def min_tail(ql=ALL_Q):
        return jnp.min(tail_ref[:, ql])

    def live_phase(j, slot, j_next):
        late_alive = min_tail(LATE_Q) < DEAD_TAIL

        @pl.when(late_alive)
        def _():
            phase(j, slot, ALL_Q, None, j_next, ALL_Q)

        @pl.when(jnp.logical_not(late_alive))
        def _():
            phase(j, slot, EARLY_Q, None, j_next, EARLY_Q)

    def cond(carry):
        u, tmin = carry
        return jnp.logical_and(u < i, tmin < DEAD_TAIL)

    def body(carry):
        u, _ = carry
        j = 2 * i - 1 - 2 * u
        live_phase(j, 0, j - 1)

        @pl.when(min_tail() < DEAD_TAIL)
        def _():
            live_phase(j - 1, 1, jnp.maximum(j - 2, 0))

        return u + 1, min_tail()

    lax.while_loop(cond, body, (jnp.int32(0), min_tail()))
    for hp in range(C_HEADS // 2):
        blk = slice(LANES * hp, LANES * (hp + 1))
        o_ref[:, blk] = acc_ref[blk].T.astype(BF16)


def _attn_c(q, k, vt, u):
    n_q = SEQ // TQ
    return pl.pallas_call(
        _attn_c_kernel, grid=(BATCH, n_q),
        in_specs=[pl.BlockSpec((None, TQ, 2 * C_WIDTH), lambda b, i: (b, i, 0)),
                  _kv_spec(C_WIDTH), _vt_spec(C_WIDTH), _resident(u.shape)],
        out_specs=pl.BlockSpec((None, TQ, C_WIDTH), lambda b, i: (b, i, 0)),
        out_shape=jax.ShapeDtypeStruct((BATCH, SEQ, C_WIDTH), BF16),
        scratch_shapes=[pltpu.VMEM((2, C_HEADS, TK, TQ), F32), pltpu.VMEM((C_WIDTH, TQ), F32),
                        pltpu.VMEM((8 * C_HEADS, TQ), F32)],
        compiler_params=_params(2), name="attn_c")(q, k, vt, u)


def _ffn_kernel(*refs):
    x_ref, part_refs = refs[0], refs[1:-7]
    wo_ref, g_ref, wg_ref, wu_ref, wd_ref, o_ref, a_ref = refs[-7:]
    x = x_ref[...]
    off = 0
    for p_ref in part_refs:
        n = p_ref.shape[1]
        x = x + jnp.dot(p_ref[...], wo_ref[off:off + n, :], preferred_element_type=F32)
        off += n
    hb = _rms(x, g_ref[...]).astype(BF16)
    for c in range(D_FF // FF_CHUNK):
        sl = slice(c * FF_CHUNK, (c + 1) * FF_CHUNK)
        g = jnp.dot(hb, wg_ref[:, sl], preferred_element_type=F32)
        u = jnp.dot(hb, wu_ref[:, sl], preferred_element_type=F32)
        a_ref[:, sl] = (g * (1.0 / (1.0 + jnp.exp(-g))) * u).astype(BF16)
    o_ref[...] = x + jnp.dot(a_ref[...], wd_ref[...], preferred_element_type=F32)


def _outproj_ffn(x2, parts, w_out, gain, wg, wu, wd):
    t = x2.shape[0]
    row = lambda n: pl.BlockSpec((TM_FFN, n), lambda i: (i, 0))
    weights = [w_out, gain, wg, wu, wd]
    return pl.pallas_call(
        _ffn_kernel, grid=(t // TM_FFN,),
        in_specs=([row(D_MODEL)] + [row(p.shape[1]) for p in parts]
                  + [_resident(w.shape) for w in weights]),
        out_specs=row(D_MODEL), out_shape=jax.ShapeDtypeStruct(x2.shape, F32),
        scratch_shapes=[pltpu.VMEM((TM_FFN, D_FF), BF16)],
        compiler_params=_params(1), name="outproj_ffn")(x2, *parts, *weights)


def _consts():
    lane = np.arange(LANES)
    pos = jnp.arange(SEQ, dtype=F32)[:, None]

    def table(half, idx, active, neg):
        inv_freq = jnp.power(ROPE_THETA, -jnp.arange(half, dtype=F32) / half)
        ang = pos * inv_freq[idx][None, :]
        act = jnp.asarray(active)[None, :]
        sign = jnp.asarray(np.where(neg, -1.0, 1.0), F32)[None, :]
        return jnp.where(act, jnp.cos(ang), 1.0), jnp.where(act, jnp.sin(ang) * sign, 0.0)

    cosa, sina = table(32, lane % 32, np.ones(LANES, bool), (lane % 64) < 32)
    rope_b = (lane >= 64) & (lane < 96)
    cosb, sinb = table(16, (lane - 64) % 16, rope_b, lane < 80)

    i256 = np.arange(MXU_DIM)
    ga = ((i256[:, None] // 64) == (i256[None, :] // 64)) / 64.0
    blk = i256 // LANES
    w = i256 % LANES
    grp = np.where(w < 64, 0, np.where(w < 96, 1, 2))
    same = (blk[:, None] == blk[None, :]) & (grp[:, None] == grp[None, :])
    scale = np.where(grp == 0, 1.0 / 64, np.where(grp == 1, 1.0 / 32, 0.0))
    gb = same * scale[None, :]
    it = np.arange(TK)
    u = (it[None, :] > it[:, None]).astype(np.float32)
    return dict(cosa=cosa, sina=sina, cosb=cosb, sinb=sinb,
                ga=jnp.asarray(ga, BF16), gb=jnp.asarray(gb, BF16), u=jnp.asarray(u, BF16))


def _pad_heads(wm, n_heads, width):
    k = wm.shape[0]
    w3 = wm.reshape(k, n_heads, width)
    return jnp.pad(w3, ((0, 0), (0, 0), (0, LANES - width))).reshape(k, n_heads * LANES)


def _tile_gain(parts, reps, scale=1.0):
    g = jnp.concatenate([p.astype(F32) for p in parts])
    g = jnp.pad(g, (0, LANES - g.shape[0]))
    return (jnp.tile(g, reps) * scale)[None, :]


def _even_weights(i, l, norm_mix, ab_w_in, a_q_norm, a_k_norm, a_out_norm, b_q_a_norm, b_w_q_b,
                  b_kv_a_norm, b_w_kv_b, b_q_nope_norm, b_q_rope_norm, b_k_nope_norm,
                  b_k_rope_norm):
    w_in = ab_w_in[i]
    kr_cols = jnp.pad(w_in[:, 1920:1952], ((0, 0), (64, 32)))
    wtok = jnp.concatenate([w_in[:, 0:1024], w_in[:, 1536:1920], kr_cols], axis=1).astype(BF16)
    wkv = b_w_kv_b[i].reshape(B_KV_RANK, B_HEADS, B_NOPE_DIM + B_V_DIM)
    sa = (A_QK_DIM ** -0.5) * LOG2E
    sb = ((B_NOPE_DIM + B_ROPE_DIM) ** -0.5) * LOG2E
    zeros32 = jnp.zeros((32,), F32)
    return dict(
        gmix=norm_mix[l][None, :],
        wtok=wtok,
        wavt=w_in[:, 1024:1536].T.astype(BF16),
        wqb=_pad_heads(b_w_q_b[i], B_HEADS, B_NOPE_DIM + B_ROPE_DIM).astype(BF16),
        wkb=_pad_heads(wkv[:, :, :B_NOPE_DIM].reshape(B_KV_RANK, -1), B_HEADS,
                       B_NOPE_DIM).astype(BF16),
        wbvt=wkv[:, :, B_NOPE_DIM:].reshape(B_KV_RANK, -1).T.astype(BF16),
        gaq=_tile_gain([a_q_norm[i], a_q_norm[i]], 4, sa),
        gak=_tile_gain([a_k_norm[i], a_k_norm[i]], 4),
        gqa=b_q_a_norm[i][None, :],
        gkva=b_kv_a_norm[i][None, :],
        gbq=_tile_gain([b_q_nope_norm[i], b_q_rope_norm[i]], 8, sb),
        gbk=_tile_gain([b_k_nope_norm[i]], 8),
        gkr=_tile_gain([jnp.zeros((64,), F32), b_k_rope_norm[i], zeros32], 1),
        gout=a_out_norm[i][None, :] * (1.0 - _lambda_init(l)),
    )


def kernel(x, norm_mix, norm_ffn, ab_w_in, a_q_norm, a_k_norm, a_lambda, a_out_norm, b_q_a_norm,
           b_w_q_b, b_kv_a_norm, b_w_kv_b, b_q_nope_norm, b_q_rope_norm, b_k_nope_norm,
           b_k_rope_norm, ab_w_out, c_w_in, c_w_out, ffn_w_gate, ffn_w_up, ffn_w_down):
    consts = _consts()
    t = BATCH * SEQ
    for l in range(DEPTH):
        i = l // 2
        x3 = x.reshape(BATCH, SEQ, D_MODEL)
        if l % 2 == 0:
            w = _even_weights(i, l, norm_mix, ab_w_in, a_q_norm, a_k_norm, a_out_norm, b_q_a_norm,
                              b_w_q_b, b_kv_a_norm, b_w_kv_b, b_q_nope_norm, b_q_rope_norm,
                              b_k_nope_norm, b_k_rope_norm)
            aq, ak, avt, bq, bk, bvt = _proj_even(x3, consts, w)
            out_a = _attn_a(aq, ak, avt, a_lambda[i], w["gout"], _lambda_init(l))
            out_b = _attn_b(bq, bk, bvt)
            parts = [out_a.reshape(t, 512), out_b.reshape(t, 512)]
            w_out = ab_w_out[i].astype(BF16)
        else:
            w_in = c_w_in[i]
            q, k, vt = _proj_odd(x3, norm_mix[l][None, :], w_in[:, :2 * C_WIDTH].astype(BF16),
                                 w_in[:, 2 * C_WIDTH:].T.astype(BF16))
            out_c = _attn_c(q, k, vt, consts["u"])
            parts = [out_c.reshape(t, C_WIDTH)]
            w_out = c_w_out[i].astype(BF16)
        x2 = _outproj_ffn(x.reshape(t, D_MODEL), parts, w_out, norm_ffn[l][None, :],
                          ffn_w_gate[l].astype(BF16), ffn_w_up[l].astype(BF16),
                          ffn_w_down[l].astype(BF16))
        x = x2.reshape(BATCH, SEQ, D_MODEL)
    return x
```

```python
import functools
import math

import jax
import jax.numpy as jnp
import numpy as np
from jax import lax
from jax.experimental import pallas as pl
from jax.experimental.pallas import tpu as pltpu

D_MODEL = 1024
BATCH = 16
SEQ = 2048
DEPTH = 4
CHUNK = 64
ROPE_THETA = 10000.0
NORM_EPS = 1e-6

A_HEADS = 4
A_QK_DIM = 64
A_V_DIM = 128
B_HEADS = 8
B_Q_RANK = 256
B_KV_RANK = 128
B_NOPE_DIM = 64
B_ROPE_DIM = 32
B_V_DIM = 64
C_HEADS = 16
C_HEAD_DIM = 64
C_WIDTH = C_HEADS * C_HEAD_DIM
D_FF = 2816

LANES = 128
MXU_DIM = 256
TQ = 512
TK = 256
TM = 512
TM_FFN = 1024
FF_CHUNK = 256
VMEM_LIMIT_BYTES = 56 * 1024 * 1024
LOG2E = 1.4426950408889634
NEG_BIG = -1e30
DEAD_TAIL = 151.0
STICK_LAG = 1

F32 = jnp.float32
BF16 = jnp.bfloat16
NT_DIMS = (((1,), (1,)), ((), ()))


def _lambda_init(layer_idx):
    return 0.8 - 0.6 * math.exp(-0.3 * layer_idx)


def _params(n_axes):
    return pltpu.CompilerParams(dimension_semantics=("arbitrary",) * n_axes,
                                vmem_limit_bytes=VMEM_LIMIT_BYTES)


def _resident(shape):
    zeros = (0,) * len(shape)
    return pl.BlockSpec(shape, lambda *_: zeros, pipeline_mode=pl.Buffered(1))


def _rms(x, gain):
    ms = jnp.mean(x * x, axis=1, keepdims=True)
    return x * lax.rsqrt(ms + NORM_EPS) * gain


def _group_mean_sq(y, g_ref):
    outs = []
    for c in range(y.shape[1] // MXU_DIM):
        yc = y[:, c * MXU_DIM:(c + 1) * MXU_DIM]
        outs.append(jnp.dot((yc * yc).astype(BF16), g_ref[...], preferred_element_type=F32))
    return jnp.concatenate(outs, axis=1)


def _group_rms(y, g_ref, gain):
    return y * lax.rsqrt(_group_mean_sq(y, g_ref) + NORM_EPS) * gain


def _rope(y, first_half, half, cos, sin_signed):
    n = y.shape[1]
    partner = jnp.where(first_half, pltpu.roll(y, n - half, 1), pltpu.roll(y, half, 1))
    return y * cos + partner * sin_signed


def _split_halves(q):
    lo_lanes = lax.broadcasted_iota(jnp.int32, (q.shape[0], LANES), 1) < 64
    blocks = []
    for b in range(q.shape[1] // LANES):
        blk = q[:, b * LANES:(b + 1) * LANES]
        blocks += [jnp.where(lo_lanes, blk, 0.0), jnp.where(lo_lanes, 0.0, blk)]
    return jnp.concatenate(blocks, axis=1)


def _proj_even_kernel(x_ref, gmix_ref, wtok_ref, wavt_ref, wqb_ref, wkb_ref, wbvt_ref,
                      ga_ref, gb_ref, gaq_ref, gak_ref, gqa_ref, gkva_ref, gbq_ref, gbk_ref,
                      gkr_ref, cosa_ref, sina_ref, cosb_ref, sinb_ref,
                      aq_ref, ak_ref, avt_ref, bq_ref, bk_ref, bvt_ref):
    tm = x_ref.shape[0]
    hb = _rms(x_ref[...], gmix_ref[...]).astype(BF16)
    p = jnp.dot(hb, wtok_ref[...], preferred_element_type=F32)

    lane_a = lax.broadcasted_iota(jnp.int32, (tm, 512), 1)
    first_a = (lane_a & 32) == 0
    cosa = jnp.concatenate([cosa_ref[...]] * 4, axis=1)
    sina = jnp.concatenate([sina_ref[...]] * 4, axis=1)
    aq = _rope(_group_rms(p[:, 0:512], ga_ref, gaq_ref[...]), first_a, 32, cosa, sina)
    aq_ref[...] = _split_halves(aq).astype(BF16)
    ak = _rope(_group_rms(p[:, 512:1024], ga_ref, gak_ref[...]), first_a, 32, cosa, sina)
    ak_ref[...] = ak.astype(BF16)
    avt = lax.dot_general(wavt_ref[...], hb, NT_DIMS, preferred_element_type=F32)
    for c in range(tm // TK):
        avt_ref[c] = avt[:, c * TK:(c + 1) * TK].astype(BF16)

    qln = _rms(p[:, 1024:1280], gqa_ref[...]).astype(BF16)
    qb = jnp.dot(qln, wqb_ref[...], preferred_element_type=F32)
    lane_b = lax.broadcasted_iota(jnp.int32, (tm, 1024), 1)
    first_b = (lane_b & 127) < 80
    cosb = jnp.concatenate([cosb_ref[...]] * 8, axis=1)
    sinb = jnp.concatenate([sinb_ref[...]] * 8, axis=1)
    bq = _rope(_group_rms(qb, gb_ref, gbq_ref[...]), first_b, 16, cosb, sinb)
    bq_ref[...] = bq.astype(BF16)

    kvn = _rms(p[:, 1280:1408], gkva_ref[...]).astype(BF16)
    kb = _group_rms(jnp.dot(kvn, wkb_ref[...], preferred_element_type=F32),
                    gb_ref, gbk_ref[...])
    kr = p[:, 1408:1536]
    krn = kr * lax.rsqrt(jnp.sum(kr * kr, axis=1, keepdims=True) * (1.0 / B_ROPE_DIM)
                         + NORM_EPS) * gkr_ref[...]
    lane_r = lax.broadcasted_iota(jnp.int32, (tm, 128), 1)
    krr = _rope(krn, lane_r < 80, 16, cosb_ref[...], sinb_ref[...])
    bk_ref[...] = (kb + jnp.concatenate([krr] * 8, axis=1)).astype(BF16)
    bvt = lax.dot_general(wbvt_ref[...], kvn, NT_DIMS, preferred_element_type=F32)
    for c in range(tm // TK):
        bvt_ref[c] = bvt[:, c * TK:(c + 1) * TK].astype(BF16)


def _proj_even(x3, consts, w):
    n_s = SEQ // TM
    grid = (BATCH * n_s,)
    row = lambda n: pl.BlockSpec((None, TM, n), lambda i: (i // n_s, i % n_s, 0))
    vt = lambda n: pl.BlockSpec((None, TM // TK, n, TK), lambda i: (i // n_s, i % n_s, 0, 0))
    tab = pl.BlockSpec((TM, LANES), lambda i: (i % n_s, 0))
    ins = [x3, w["gmix"], w["wtok"], w["wavt"], w["wqb"], w["wkb"], w["wbvt"],
           consts["ga"], consts["gb"], w["gaq"], w["gak"], w["gqa"], w["gkva"], w["gbq"],
           w["gbk"], w["gkr"], consts["cosa"], consts["sina"], consts["cosb"], consts["sinb"]]
    in_specs = [row(D_MODEL)] + [_resident(a.shape) for a in ins[1:16]] + [tab] * 4
    out_shape = [
        jax.ShapeDtypeStruct((BATCH, SEQ, 1024), BF16),
        jax.ShapeDtypeStruct((BATCH, SEQ, 512), BF16),
        jax.ShapeDtypeStruct((BATCH, SEQ // TK, 512, TK), BF16),
        jax.ShapeDtypeStruct((BATCH, SEQ, 1024), BF16),
        jax.ShapeDtypeStruct((BATCH, SEQ, 1024), BF16),
        jax.ShapeDtypeStruct((BATCH, SEQ // TK, 512, TK), BF16),
    ]
    out_specs = [row(1024), row(512), vt(512), row(1024), row(1024), vt(512)]
    return pl.pallas_call(_proj_even_kernel, grid=grid, in_specs=in_specs, out_specs=out_specs,
                          out_shape=out_shape, compiler_params=_params(1),
                          name="proj_even")(*ins)


def _proj_odd_kernel(x_ref, gmix_ref, wqk_ref, wvt_ref, q_ref, k_ref, vt_ref):
    tm = x_ref.shape[0]
    hb = _rms(x_ref[...], gmix_ref[...]).astype(BF16)
    qk = jnp.dot(hb, wqk_ref[...], preferred_element_type=F32)
    q_ref[...] = _split_halves(qk[:, :C_WIDTH] * (C_HEAD_DIM ** -0.5 * LOG2E)).astype(BF16)
    k_ref[...] = qk[:, C_WIDTH:].astype(BF16)
    vt = lax.dot_general(wvt_ref[...], hb, NT_DIMS, preferred_element_type=F32)
    for c in range(tm // TK):
        vt_ref[c] = vt[:, c * TK:(c + 1) * TK].astype(BF16)


def _proj_odd(x3, gmix, wqk, wvt):
    n_s = SEQ // TM
    row = lambda n: pl.BlockSpec((None, TM, n), lambda i: (i // n_s, i % n_s, 0))
    vt = pl.BlockSpec((None, TM // TK, C_WIDTH, TK), lambda i: (i // n_s, i % n_s, 0, 0))
    out_shape = [
        jax.ShapeDtypeStruct((BATCH, SEQ, 2 * C_WIDTH), BF16),
        jax.ShapeDtypeStruct((BATCH, SEQ, C_WIDTH), BF16),
        jax.ShapeDtypeStruct((BATCH, SEQ // TK, C_WIDTH, TK), BF16),
    ]
    return pl.pallas_call(
        _proj_odd_kernel, grid=(BATCH * n_s,),
        in_specs=[row(D_MODEL), _resident(gmix.shape), _resident(wqk.shape), _resident(wvt.shape)],
        out_specs=[row(2 * C_WIDTH), row(C_WIDTH), vt], out_shape=out_shape,
        compiler_params=_params(1), name="proj_odd")(x3, gmix, wqk, wvt)


def _diag_masks(fn):
    r = lax.broadcasted_iota(jnp.int32, (TK, TQ), 0)
    c = lax.broadcasted_iota(jnp.int32, (TK, TQ), 1)
    return fn(r, c), fn(r + TK, c)


A_ACC = A_V_DIM + 16
B_ACC = B_V_DIM + 16
SUM_ROWS = 16


def _softmax_rest(s, vt, mask, tmax, c, ql, m_ref, acc_ref, rows):
    if mask is not None:
        s = jnp.where(mask, s, -jnp.inf)
    if tmax is None:
        tmax = jnp.max(s, axis=0, keepdims=True)
    row = slice(8 * c, 8 * c + 1)
    m = m_ref[row, ql]
    m_new = jnp.maximum(m, tmax)
    alpha = jnp.exp2(m - m_new)
    p = jnp.exp2(s - m_new)
    m_ref[row, ql] = m_new
    ones_row = lax.broadcasted_iota(jnp.int32, (SUM_ROWS, vt.shape[1]), 0) == 0
    vt_aug = jnp.concatenate([vt, jnp.where(ones_row, 1.0, 0.0).astype(BF16)], axis=0)
    acc_ref[rows, ql] = (alpha * acc_ref[rows, ql]
                         + jnp.dot(vt_aug, p.astype(BF16), preferred_element_type=F32))


ALL_Q = slice(0, TQ)
EARLY_Q = slice(0, TQ // 2)
LATE_Q = slice(TQ // 2, TQ)


def _softmax_attention(i, n, scores, rest, s_ref, t_ref, m_ref, acc_ref):
    acc_ref[...] = jnp.zeros_like(acc_ref)
    m_ref[...] = jnp.full_like(m_ref, NEG_BIG)

    def phase(j, slot, parts, j_next, next_ql):
        for c in range(n):
            nxt = scores(j_next, c, next_ql)
            s_ref[1 - slot, c, :, next_ql] = nxt
            t_ref[1 - slot, 8 * c:8 * c + 1, next_ql] = jnp.max(nxt, axis=0, keepdims=True)
            for ql, mask in parts:
                tmax = None if mask is not None else t_ref[slot, 8 * c:8 * c + 1, ql]
                rest(j, c, ql, s_ref[slot, c, :, ql], None if mask is None else mask[:, ql], tmax)

    for c in range(n):
        s_ref[0, c, :, LATE_Q] = scores(2 * i + 1, c, LATE_Q)
    mask0, mask1 = _diag_masks(lambda kp, qp: (kp >> 6) <= (qp >> 6))
    phase(2 * i + 1, 0, [(LATE_Q, mask1)], 2 * i, ALL_Q)
    phase(2 * i, 1, [(EARLY_Q, mask0), (LATE_Q, None)], 0, ALL_Q)

    def body(u, carry):
        phase(2 * u, 0, [(ALL_Q, None)], 2 * u + 1, ALL_Q)
        phase(2 * u + 1, 1, [(ALL_Q, None)], jnp.minimum(2 * u + 2, 2 * i - 1), ALL_Q)
        return carry

    lax.fori_loop(0, i, body, 0)


def _attn_a_kernel(q_ref, k_ref, vt_ref, lam_ref, gout_ref, o_ref, s_ref, t_ref, acc_ref, m_ref,
                   *, lam0):
    i = pl.program_id(1)

    def scores(j, c, ql):
        h = c // 2
        k = k_ref[pl.ds(pl.multiple_of(j * TK, TK), TK), LANES * h:LANES * (h + 1)]
        return lax.dot_general(k, q_ref[ql, LANES * c:LANES * (c + 1)], NT_DIMS,
                               preferred_element_type=F32)

    def rest(j, c, ql, s, mask, tmax):
        h = c // 2
        _softmax_rest(s, vt_ref[j, LANES * h:LANES * (h + 1), :], mask, tmax, c, ql, m_ref,
                      acc_ref, slice(A_ACC * c, A_ACC * (c + 1)))

    _softmax_attention(i, 2 * A_HEADS, scores, rest, s_ref, t_ref, m_ref, acc_ref)
    al = lam_ref[...]
    lam = (jnp.exp(jnp.sum(al[0:1] * al[1:2], axis=1, keepdims=True))
           - jnp.exp(jnp.sum(al[2:3] * al[3:4], axis=1, keepdims=True)) + lam0)
    for h in range(A_HEADS):
        outs = []
        for c in (2 * h, 2 * h + 1):
            num = acc_ref[A_ACC * c:A_ACC * c + A_V_DIM]
            den = acc_ref[A_ACC * c + A_V_DIM:A_ACC * c + A_V_DIM + 1]
            outs.append(num * (1.0 / den))
        o1, o2 = outs
        o = o1 - lam * o2
        o = o * lax.rsqrt(jnp.mean(o * o, axis=0, keepdims=True) + NORM_EPS)
        o_ref[:, LANES * h:LANES * (h + 1)] = (o.T * gout_ref[...]).astype(BF16)


def _kv_spec(n_lanes):
    return pl.BlockSpec((None, SEQ, n_lanes), lambda b, i: (b, 0, 0))


def _vt_spec(n_rows):
    return pl.BlockSpec((None, SEQ // TK, n_rows, TK), lambda b, i: (b, 0, 0, 0))


def _attn_a(aq, ak, avt, a_lambda, gout, lam0):
    n_q = SEQ // TQ
    n_chain = 2 * A_HEADS
    kern = functools.partial(_attn_a_kernel, lam0=lam0)
    return pl.pallas_call(
        kern, grid=(BATCH, n_q),
        in_specs=[pl.BlockSpec((None, TQ, n_chain * LANES), lambda b, i: (b, i, 0)),
                  _kv_spec(512), _vt_spec(512),
                  _resident(a_lambda.shape), _resident(gout.shape)],
        out_specs=pl.BlockSpec((None, TQ, 512), lambda b, i: (b, i, 0)),
        out_shape=jax.ShapeDtypeStruct((BATCH, SEQ, 512), BF16),
        scratch_shapes=[pltpu.VMEM((2, n_chain, TK, TQ), F32), pltpu.VMEM((2, 8 * n_chain, TQ), F32),
                        pltpu.VMEM((n_chain * A_ACC, TQ), F32), pltpu.VMEM((8 * n_chain, TQ), F32)],
        compiler_params=_params(2), name="attn_a")(aq, ak, avt, a_lambda, gout)


def _attn_b_kernel(q_ref, k_ref, vt_ref, o_ref, s_ref, t_ref, acc_ref, m_ref):
    i = pl.program_id(1)

    def scores(j, h, ql):
        cols = slice(LANES * h, LANES * (h + 1))
        return lax.dot_general(k_ref[pl.ds(pl.multiple_of(j * TK, TK), TK), cols], q_ref[ql, cols],
                               NT_DIMS, preferred_element_type=F32)

    def rest(j, h, ql, s, mask, tmax):
        _softmax_rest(s, vt_ref[j, B_V_DIM * h:B_V_DIM * (h + 1), :], mask, tmax, h, ql, m_ref,
                      acc_ref, slice(B_ACC * h, B_ACC * (h + 1)))

    _softmax_attention(i, B_HEADS, scores, rest, s_ref, t_ref, m_ref, acc_ref)
    for hp in range(B_HEADS // 2):
        parts = []
        for h in (2 * hp, 2 * hp + 1):
            num = acc_ref[B_ACC * h:B_ACC * h + B_V_DIM]
            den = acc_ref[B_ACC * h + B_V_DIM:B_ACC * h + B_V_DIM + 1]
            parts.append(num * (1.0 / den))
        o_ref[:, LANES * hp:LANES * (hp + 1)] = jnp.concatenate(parts, axis=0).T.astype(BF16)


def _attn_b(bq, bk, bvt):
    n_q = SEQ // TQ
    return pl.pallas_call(
        _attn_b_kernel, grid=(BATCH, n_q),
        in_specs=[pl.BlockSpec((None, TQ, 1024), lambda b, i: (b, i, 0)),
                  _kv_spec(1024), _vt_spec(512)],
        out_specs=pl.BlockSpec((None, TQ, 512), lambda b, i: (b, i, 0)),
        out_shape=jax.ShapeDtypeStruct((BATCH, SEQ, 512), BF16),
        scratch_shapes=[pltpu.VMEM((2, B_HEADS, TK, TQ), F32), pltpu.VMEM((2, 8 * B_HEADS, TQ), F32),
                        pltpu.VMEM((B_HEADS * B_ACC, TQ), F32), pltpu.VMEM((8 * B_HEADS, TQ), F32)],
        compiler_params=_params(2), name="attn_b")(bq, bk, bvt)


def _stick_front(z, u_ref, mask):
    log1p2 = jnp.log(1.0 + jnp.exp2(jnp.minimum(z, 126.0))) * LOG2E
    sp = jnp.maximum(z, log1p2)
    logsig = z - sp
    if mask is not None:
        sp = jnp.where(mask, sp, 0.0)
    tl = jnp.dot(u_ref[...], sp.astype(BF16), preferred_element_type=F32)
    return logsig, tl[0:1, :] + sp[0:1, :], tl


def _stick_back(front, vt, mask, c, tail_ref, acc_ref, rows, ql):
    logsig, tile_mass, tl = front
    row = slice(8 * c, 8 * c + 1)
    tail = tail_ref[row, ql]
    w = jnp.exp2(logsig - tl)
    if mask is not None:
        w = jnp.where(mask, w, 0.0)
    pv = jnp.dot(vt, w.astype(BF16), preferred_element_type=F32)
    acc_ref[rows, ql] = acc_ref[rows, ql] + pv * jnp.exp2(-tail)
    tail_ref[row, ql] = tail + tile_mass


def _attn_c_kernel(q_ref, k_ref, vt_ref, u_ref, o_ref, s_ref, acc_ref, tail_ref):
    i = pl.program_id(1)
    n = C_HEADS

    def scores(j, h, ql):
        k = k_ref[pl.ds(pl.multiple_of(j * TK, TK), TK), LANES * (h // 2):LANES * (h // 2 + 1)]
        return lax.dot_general(k, q_ref[ql, LANES * h:LANES * (h + 1)], NT_DIMS,
                               preferred_element_type=F32)

    def phase(j, slot, ql, mask, j_next, next_ql=ALL_Q):
        fronts = {}
        qmask = None if mask is None else mask[:, ql]
        for c in range(n + STICK_LAG):
            if c < n:
                fronts[c] = _stick_front(s_ref[slot, c, :, ql], u_ref, qmask)
                s_ref[1 - slot, c, :, next_ql] = scores(j_next, c, next_ql)
            d = c - STICK_LAG
            if d >= 0:
                rows = slice(C_HEAD_DIM * d, C_HEAD_DIM * (d + 1))
                _stick_back(fronts.pop(d), vt_ref[j, rows, :], qmask, d, tail_ref, acc_ref,
                            rows, ql)

    acc_ref[...] = jnp.zeros_like(acc_ref)
    state_row = lax.broadcasted_iota(jnp.int32, tail_ref.shape, 0)
    tail_ref[...] = jnp.where((state_row & 7) == 0, 0.0, -NEG_BIG)
    for c in range(n):
        s_ref[0, c, :, LATE_Q] = scores(2 * i + 1, c, LATE_Q)
    mask0, mask1 = _diag_masks(lambda kp, qp: kp < qp)
    phase(2 * i + 1, 0, LATE_Q, mask1, 2 * i)
    phase(2 * i, 1, ALL_Q, mask0, jnp.maximum(2 * i - 1, 0))

    def min_tail(ql=ALL_Q):
        return jnp.min(tail_ref[:, ql])

    def live_phase(j, slot, j_next):
        late_alive = min_tail(LATE_Q) < DEAD_TAIL

        @pl.when(late_alive)
        def _():
            phase(j, slot, ALL_Q, None, j_next, ALL_Q)

        @pl.when(jnp.logical_not(late_alive))
        def _():
            phase(j, slot, EARLY_Q, None, j_next, EARLY_Q)

    def cond(carry):
        u, tmin = carry
        return jnp.logical_and(u < i, tmin < DEAD_TAIL)

    def body(carry):
        u, _ = carry
        j = 2 * i - 1 - 2 * u
        live_phase(j, 0, j - 1)

        @pl.when(min_tail() < DEAD_TAIL)
        def _():
            live_phase(j - 1, 1, jnp.maximum(j - 2, 0))

        return u + 1, min_tail()

    lax.while_loop(cond, body, (jnp.int32(0), min_tail()))
    for hp in range(C_HEADS // 2):
        blk = slice(LANES * hp, LANES * (hp + 1))
        o_ref[:, blk] = acc_ref[blk].T.astype(BF16)


def _attn_c(q, k, vt, u):
    n_q = SEQ // TQ
    return pl.pallas_call(
        _attn_c_kernel, grid=(BATCH, n_q),
        in_specs=[pl.BlockSpec((None, TQ, 2 * C_WIDTH), lambda b, i: (b, i, 0)),
                  _kv_spec(C_WIDTH), _vt_spec(C_WIDTH), _resident(u.shape)],
        out_specs=pl.BlockSpec((None, TQ, C_WIDTH), lambda b, i: (b, i, 0)),
        out_shape=jax.ShapeDtypeStruct((BATCH, SEQ, C_WIDTH), BF16),
        scratch_shapes=[pltpu.VMEM((2, C_HEADS, TK, TQ), F32), pltpu.VMEM((C_WIDTH, TQ), F32),
                        pltpu.VMEM((8 * C_HEADS, TQ), F32)],
        compiler_params=_params(2), name="attn_c")(q, k, vt, u)


def _ffn_kernel(*refs):
    x_ref, part_refs = refs[0], refs[1:-7]
    wo_ref, g_ref, wg_ref, wu_ref, wd_ref, o_ref, a_ref = refs[-7:]
    x = x_ref[...]
    off = 0
    for p_ref in part_refs:
        n = p_ref.shape[1]
        x = x + jnp.dot(p_ref[...], wo_ref[off:off + n, :], preferred_element_type=F32)
        off += n
    hb = _rms(x, g_ref[...]).astype(BF16)
    for c in range(D_FF // FF_CHUNK):
        sl = slice(c * FF_CHUNK, (c + 1) * FF_CHUNK)
        g = jnp.dot(hb, wg_ref[:, sl], preferred_element_type=F32)
        u = jnp.dot(hb, wu_ref[:, sl], preferred_element_type=F32)
        a_ref[:, sl] = (g * (1.0 / (1.0 + jnp.exp(-g))) * u).astype(BF16)
    o_ref[...] = x + jnp.dot(a_ref[...], wd_ref[...], preferred_element_type=F32)


def _outproj_ffn(x2, parts, w_out, gain, wg, wu, wd):
    t = x2.shape[0]
    row = lambda n: pl.BlockSpec((TM_FFN, n), lambda i: (i, 0))
    weights = [w_out, gain, wg, wu, wd]
    return pl.pallas_call(
        _ffn_kernel, grid=(t // TM_FFN,),
        in_specs=([row(D_MODEL)] + [row(p.shape[1]) for p in parts]
                  + [_resident(w.shape) for w in weights]),
        out_specs=row(D_MODEL), out_shape=jax.ShapeDtypeStruct(x2.shape, F32),
        scratch_shapes=[pltpu.VMEM((TM_FFN, D_FF), BF16)],
        compiler_params=_params(1), name="outproj_ffn")(x2, *parts, *weights)


def _consts():
    lane = np.arange(LANES)
    pos = jnp.arange(SEQ, dtype=F32)[:, None]

    def table(half, idx, active, neg):
        inv_freq = jnp.power(ROPE_THETA, -jnp.arange(half, dtype=F32) / half)
        ang = pos * inv_freq[idx][None, :]
        act = jnp.asarray(active)[None, :]
        sign = jnp.asarray(np.where(neg, -1.0, 1.0), F32)[None, :]
        return jnp.where(act, jnp.cos(ang), 1.0), jnp.where(act, jnp.sin(ang) * sign, 0.0)

    cosa, sina = table(32, lane % 32, np.ones(LANES, bool), (lane % 64) < 32)
    rope_b = (lane >= 64) & (lane < 96)
    cosb, sinb = table(16, (lane - 64) % 16, rope_b, lane < 80)

    i256 = np.arange(MXU_DIM)
    ga = ((i256[:, None] // 64) == (i256[None, :] // 64)) / 64.0
    blk = i256 // LANES
    w = i256 % LANES
    grp = np.where(w < 64, 0, np.where(w < 96, 1, 2))
    same = (blk[:, None] == blk[None, :]) & (grp[:, None] == grp[None, :])
    scale = np.where(grp == 0, 1.0 / 64, np.where(grp == 1, 1.0 / 32, 0.0))
    gb = same * scale[None, :]
    it = np.arange(TK)
    u = (it[None, :] > it[:, None]).astype(np.float32)
    return dict(cosa=cosa, sina=sina, cosb=cosb, sinb=sinb,
                ga=jnp.asarray(ga, BF16), gb=jnp.asarray(gb, BF16), u=jnp.asarray(u, BF16))


def _pad_heads(wm, n_heads, width):
    k = wm.shape[0]
    w3 = wm.reshape(k, n_heads, width)
    return jnp.pad(w3, ((0, 0), (0, 0), (0, LANES - width))).reshape(k, n_heads * LANES)


def _tile_gain(parts, reps, scale=1.0):
    g = jnp.concatenate([p.astype(F32) for p in parts])
    g = jnp.pad(g, (0, LANES - g.shape[0]))
    return (jnp.tile(g, reps) * scale)[None, :]


def _even_weights(i, l, norm_mix, ab_w_in, a_q_norm, a_k_norm, a_out_norm, b_q_a_norm, b_w_q_b,
                  b_kv_a_norm, b_w_kv_b, b_q_nope_norm, b_q_rope_norm, b_k_nope_norm,
                  b_k_rope_norm):
    w_in = ab_w_in[i]
    kr_cols = jnp.pad(w_in[:, 1920:1952], ((0, 0), (64, 32)))
    wtok = jnp.concatenate([w_in[:, 0:1024], w_in[:, 1536:1920], kr_cols], axis=1).astype(BF16)
    wkv = b_w_kv_b[i].reshape(B_KV_RANK, B_HEADS, B_NOPE_DIM + B_V_DIM)
    sa = (A_QK_DIM ** -0.5) * LOG2E
    sb = ((B_NOPE_DIM + B_ROPE_DIM) ** -0.5) * LOG2E
    zeros32 = jnp.zeros((32,), F32)
    return dict(
        gmix=norm_mix[l][None, :],
        wtok=wtok,
        wavt=w_in[:, 1024:1536].T.astype(BF16),
        wqb=_pad_heads(b_w_q_b[i], B_HEADS, B_NOPE_DIM + B_ROPE_DIM).astype(BF16),
        wkb=_pad_heads(wkv[:, :, :B_NOPE_DIM].reshape(B_KV_RANK, -1), B_HEADS,
                       B_NOPE_DIM).astype(BF16),
        wbvt=wkv[:, :, B_NOPE_DIM:].reshape(B_KV_RANK, -1).T.astype(BF16),
        gaq=_tile_gain([a_q_norm[i], a_q_norm[i]], 4, sa),
        gak=_tile_gain([a_k_norm[i], a_k_norm[i]], 4),
        gqa=b_q_a_norm[i][None, :],
        gkva=b_kv_a_norm[i][None, :],
        gbq=_tile_gain([b_q_nope_norm[i], b_q_rope_norm[i]], 8, sb),
        gbk=_tile_gain([b_k_nope_norm[i]], 8),
        gkr=_tile_gain([jnp.zeros((64,), F32), b_k_rope_norm[i], zeros32], 1),
        gout=a_out_norm[i][None, :] * (1.0 - _lambda_init(l)),
    )


def kernel(x, norm_mix, norm_ffn, ab_w_in, a_q_norm, a_k_norm, a_lambda, a_out_norm, b_q_a_norm,
           b_w_q_b, b_kv_a_norm, b_w_kv_b, b_q_nope_norm, b_q_rope_norm, b_k_nope_norm,
           b_k_rope_norm, ab_w_out, c_w_in, c_w_out, ffn_w_gate, ffn_w_up, ffn_w_down):
    consts = _consts()
    t = BATCH * SEQ
    for l in range(DEPTH):
        i = l // 2
        x3 = x.reshape(BATCH, SEQ, D_MODEL)
        if l % 2 == 0:
            w = _even_weights(i, l, norm_mix, ab_w_in, a_q_norm, a_k_norm, a_out_norm, b_q_a_norm,
                              b_w_q_b, b_kv_a_norm, b_w_kv_b, b_q_nope_norm, b_q_rope_norm,
                              b_k_nope_norm, b_k_rope_norm)
            aq, ak, avt, bq, bk, bvt = _proj_even(x3, consts, w)
            out_a = _attn_a(aq, ak, avt, a_lambda[i], w["gout"], _lambda_init(l))
            out_b = _attn_b(bq, bk, bvt)
            parts = [out_a.reshape(t, 512), out_b.reshape(t, 512)]
            w_out = ab_w_out[i].astype(BF16)
        else:
            w_in = c_w_in[i]
            q, k, vt = _proj_odd(x3, norm_mix[l][None, :], w_in[:, :2 * C_WIDTH].astype(BF16),
                                 w_in[:, 2 * C_WIDTH:].T.astype(BF16))
            out_c = _attn_c(q, k, vt, consts["u"])
            parts = [out_c.reshape(t, C_WIDTH)]
            w_out = c_w_out[i].astype(BF16)
        x2 = _outproj_ffn(x.reshape(t, D_MODEL), parts, w_out, norm_ffn[l][None, :],
                          ffn_w_gate[l].astype(BF16), ffn_w_up[l].astype(BF16),
                          ffn_w_down[l].astype(BF16))
        x = x2.reshape(BATCH, SEQ, D_MODEL)
    return x
```
